```python
import math
import jax, jax.numpy as jnp
from jax import lax
import numpy as np

D_MODEL = 1024
BATCH = 16
SEQ = 4096
DEPTH = 1
DEC_BATCH = 128
DEC_SEQ = 4
PAST_LEN = 8192
PAGE_SIZE = 128

HEAD_DIM = 64
H_SB = 8
H_NSA = 8
G_NSA = 2
R_NSA = H_NSA // G_NSA
D_SB = H_SB * HEAD_DIM
D_NSA = H_NSA * HEAD_DIM
D_KV = G_NSA * HEAD_DIM
D_IN = 3 * D_SB + D_NSA + 6 * D_KV + 3 * H_NSA
CMP_BLOCK = 32
SEL_BLOCK = 64
TOP_N = 16
WINDOW = 512
N_BUCKETS = 32
MAX_DISTANCE = 128
N_EXPERTS = 32
TOP_K = 4
D_FF = D_MODEL
SWIGLU_LIMIT = 7.0
SWIGLU_ALPHA = 1.702
EPS = 1e-6
Q_BLOCK = 128
NSA_Q_BLOCK = 32
MOE_BLOCK = 128
MASK_VALUE = -1e30

kernel_name = 'hymba_sb_nsa_moe_decode_step'


def rmsnorm(x, g):
    xf = x.astype(jnp.float32)
    y = xf * lax.rsqrt(jnp.mean(xf * xf, axis=-1, keepdims=True) + EPS)
    return (y * g.astype(jnp.float32)).astype(x.dtype)


def masked_softmax(logits, mask):
    logits = jnp.where(mask, logits.astype(jnp.float32), MASK_VALUE)
    p = jnp.where(mask, jnp.exp(logits - jnp.max(logits, axis=-1, keepdims=True)), 0.0)
    return p / jnp.maximum(jnp.sum(p, axis=-1, keepdims=True), 1e-30)


def t5_bucket(dist):
    n = jnp.maximum(dist, 0)
    max_exact = N_BUCKETS // 2
    log_ratio = jnp.log(jnp.maximum(n, 1).astype(jnp.float32) / max_exact) / math.log(MAX_DISTANCE / max_exact)
    large = jnp.minimum(max_exact + (log_ratio * (N_BUCKETS - max_exact)).astype(jnp.int32), N_BUCKETS - 1)
    return jnp.where(n < max_exact, n, large)


def adaln(c, w_ada, b_ada):
    mod = jax.nn.silu(c) @ w_ada + b_ada
    return [m[:, None, :] for m in jnp.split(mod, 6, axis=-1)]


def modulate(x, g, shift, scale):
    return rmsnorm(x, g) * (1 + scale) + shift


def project(h, w_in):
    B, T = h.shape[:2]
    bounds = [int(b) for b in np.cumsum([D_SB, 2 * D_SB, D_NSA, 2 * D_KV, 2 * D_KV, 2 * D_KV])]
    sb_q, sb_kv, nsa_q, cmp_kv, sel_kv, win_kv, gate = jnp.split(h @ w_in, bounds, axis=-1)
    kv_shape = (B, T, 2, G_NSA, HEAD_DIM)
    return (sb_q.reshape(B, T, H_SB, HEAD_DIM), sb_kv.reshape(B, T, 2, H_SB, HEAD_DIM),
            nsa_q.reshape(B, T, H_NSA, HEAD_DIM), cmp_kv.reshape(kv_shape), sel_kv.reshape(kv_shape),
            win_kv.reshape(kv_shape), jax.nn.sigmoid(gate).reshape(B, T, H_NSA, 3))


def gather_pages(cache, layer, page_table, kv=None):
    rows = cache[layer, page_table] if kv is None else cache[layer, page_table, :, kv]
    b, n, p = rows.shape[:3]
    return rows.reshape((b, n * p) + rows.shape[3:])


def sb_weights(z, causal):
    log_keep = jnp.where(causal, jax.nn.log_sigmoid(-z), 0.0)
    between = lax.cumsum(log_keep, axis=z.ndim - 1, reverse=True) - log_keep
    return jnp.where(causal, jnp.exp(jax.nn.log_sigmoid(z) + between), 0.0)


def sb_prompt(q, kv):
    B, T = q.shape[:2]
    nb = T // Q_BLOCK
    pos = jnp.arange(T, dtype=jnp.int32)
    k, v = kv[:, :, 0], kv[:, :, 1]

    def block(args):
        qb, pq = args
        z = jnp.einsum('bqhd,bkhd->bhqk', qb, k).astype(jnp.float32) * HEAD_DIM ** -0.5
        a = sb_weights(z, pos[None, :] < pq[:, None])
        return jnp.einsum('bhqk,bkhd->bqhd', a.astype(v.dtype), v)

    qb = q.reshape(B, nb, Q_BLOCK, H_SB, HEAD_DIM).swapaxes(0, 1)
    out = lax.map(block, (qb, pos.reshape(nb, Q_BLOCK)))
    return out.swapaxes(0, 1).reshape(B, T, H_SB, HEAD_DIM)


def sb_sample(q, kv_new, cache_sb_kv, layer, page_table, pos_q):
    past = page_table.shape[1] * cache_sb_kv.shape[2]
    Ts = q.shape[1]
    z = jnp.concatenate([
        jnp.einsum('bqhd,bkhd->bhqk', q, gather_pages(cache_sb_kv, layer, page_table, 0)),
        jnp.einsum('bqhd,bkhd->bhqk', q, kv_new[:, :, 0])], axis=-1).astype(jnp.float32) * HEAD_DIM ** -0.5
    pos_k = jnp.arange(past + Ts, dtype=jnp.int32)
    a = sb_weights(z, pos_k[None, :] < pos_q[:, None]).astype(q.dtype)
    return (jnp.einsum('bhqk,bkhd->bqhd', a[..., :past], gather_pages(cache_sb_kv, layer, page_table, 1))
            + jnp.einsum('bhqk,bkhd->bqhd', a[..., past:], kv_new[:, :, 1]))


def compress_blocks(kv, w_pos):
    B, L = kv.shape[:2]
    blocks = kv.reshape(B, L // CMP_BLOCK, CMP_BLOCK, 2, G_NSA, HEAD_DIM)
    return jnp.einsum('bnlcgd,cl->bncgd', blocks, w_pos)


def nsa_cmp_sel(q, pos_q, cmp_blocks, sel_blocks, rel_bias):
    B, Tq = q.shape[:2]
    n_cmp, n_sel = cmp_blocks.shape[1], sel_blocks.shape[1]
    scale = HEAD_DIM ** -0.5
    qg = q.reshape(B, Tq, G_NSA, R_NSA, HEAD_DIM)
    blk_end = (jnp.arange(n_cmp, dtype=jnp.int32) + 1) * CMP_BLOCK - 1
    dist_c = pos_q[:, None] - blk_end[None, :]
    bias_c = rel_bias[t5_bucket(dist_c)].reshape(Tq, n_cmp, G_NSA, R_NSA).transpose(2, 3, 0, 1)
    logit_c = jnp.einsum('bqgrd,bngd->bgrqn', qg, cmp_blocks[:, :, 0]).astype(jnp.float32) * scale + bias_c
    p_c = masked_softmax(logit_c, dist_c >= 0)
    o_cmp = jnp.einsum('bgrqn,bngd->bqgrd', p_c.astype(q.dtype), cmp_blocks[:, :, 1])
    imp = p_c.sum(axis=2).reshape(B, G_NSA, Tq, n_sel, SEL_BLOCK // CMP_BLOCK).sum(-1)
    blk = jnp.arange(n_sel, dtype=jnp.int32)
    cur = (pos_q[:, None] // SEL_BLOCK) == blk[None, :]
    future = blk[None, :] * SEL_BLOCK > pos_q[:, None]
    imp = jnp.where(cur, jnp.inf, jnp.where(future, -jnp.inf, imp))
    _, idx = lax.top_k(imp, min(TOP_N, n_sel))
    b_ix = jnp.arange(B)[:, None, None, None]
    g_ix = jnp.arange(G_NSA)[None, :, None, None]
    kv_sel = sel_blocks[b_ix, idx, :, :, g_ix]
    n_k = idx.shape[-1] * SEL_BLOCK
    pos_k = (idx[..., None] * SEL_BLOCK + jnp.arange(SEL_BLOCK, dtype=jnp.int32)).reshape(B, G_NSA, Tq, n_k)
    dist_s = pos_q[:, None] - pos_k
    bias_s = jnp.moveaxis(rel_bias.T.reshape(G_NSA, R_NSA, N_BUCKETS)[g_ix, :, t5_bucket(dist_s)], -1, 3)
    k_sel = kv_sel[..., 0, :].reshape(B, G_NSA, Tq, n_k, HEAD_DIM)
    v_sel = kv_sel[..., 1, :].reshape(B, G_NSA, Tq, n_k, HEAD_DIM)
    logit_s = jnp.einsum('bqgrd,bgqnd->bgqrn', qg, k_sel).astype(jnp.float32) * scale + bias_s
    p_s = masked_softmax(logit_s, (dist_s >= 0)[:, :, :, None, :])
    o_sel = jnp.einsum('bgqrn,bgqnd->bqgrd', p_s.astype(q.dtype), v_sel)
    return o_cmp.reshape(B, Tq, H_NSA, HEAD_DIM), o_sel.reshape(B, Tq, H_NSA, HEAD_DIM)


def nsa_prompt(q, cmp_kv, sel_kv, w_pos, rel_bias):
    B, T = q.shape[:2]
    nb = T // NSA_Q_BLOCK
    cmp_blocks = compress_blocks(cmp_kv, w_pos)
    sel_blocks = sel_kv.reshape(B, T // SEL_BLOCK, SEL_BLOCK, 2, G_NSA, HEAD_DIM)
    qb = q.reshape(B, nb, NSA_Q_BLOCK, H_NSA, HEAD_DIM).swapaxes(0, 1)
    pos = jnp.arange(T, dtype=jnp.int32).reshape(nb, NSA_Q_BLOCK)
    o_cmp, o_sel = lax.map(lambda a: nsa_cmp_sel(a[0], a[1], cmp_blocks, sel_blocks, rel_bias), (qb, pos))
    unblock = lambda o: o.swapaxes(0, 1).reshape(B, T, H_NSA, HEAD_DIM)
    return unblock(o_cmp), unblock(o_sel)


def nsa_sample(q, pos_q, cmp_kv, sel_kv, cache_cmp_kv, cache_sel_kv, layer, page_table, w_pos, rel_bias):
    B, Ts = q.shape[:2]
    past = page_table.shape[1] * cache_cmp_kv.shape[2]
    pad = (-(past + Ts)) % SEL_BLOCK

    def full_rows(cache, new):
        return jnp.concatenate([gather_pages(cache, layer, page_table), new,
                                jnp.zeros((B, pad) + new.shape[2:], new.dtype)], axis=1)

    sel_rows = full_rows(cache_sel_kv, sel_kv)
    sel_blocks = sel_rows.reshape(B, sel_rows.shape[1] // SEL_BLOCK, SEL_BLOCK, 2, G_NSA, HEAD_DIM)
    return nsa_cmp_sel(q, pos_q, compress_blocks(full_rows(cache_cmp_kv, cmp_kv), w_pos), sel_blocks, rel_bias)


def window_attend(q, pos_q, kv, pos_k, rel_bias):
    B, Tq = q.shape[:2]
    Tk = kv.shape[1]
    qg = q.reshape(B, Tq, G_NSA, R_NSA, HEAD_DIM)
    dist = pos_q[:, None] - pos_k[None, :]
    mask = (dist >= 0) & (dist < WINDOW) & (pos_k[None, :] >= 0)
    bias = rel_bias[t5_bucket(dist)].reshape(Tq, Tk, G_NSA, R_NSA).transpose(2, 3, 0, 1)
    logit = jnp.einsum('bqgrd,bkgd->bgrqk', qg, kv[:, :, 0]).astype(jnp.float32) * HEAD_DIM ** -0.5 + bias
    p = masked_softmax(logit, mask)
    return jnp.einsum('bgrqk,bkgd->bqgrd', p.astype(q.dtype), kv[:, :, 1]).reshape(B, Tq, H_NSA, HEAD_DIM)


def window_prompt(q, kv, rel_bias):
    B, T = q.shape[:2]
    nb = T // Q_BLOCK
    band = WINDOW + Q_BLOCK
    kv_pad = jnp.pad(kv, ((0, 0), (WINDOW, 0), (0, 0), (0, 0), (0, 0)))

    def block(args):
        qb, s0 = args
        kv_band = lax.dynamic_slice_in_dim(kv_pad, s0, band, axis=1)
        pos_k = s0 - WINDOW + jnp.arange(band, dtype=jnp.int32)
        return window_attend(qb, s0 + jnp.arange(Q_BLOCK, dtype=jnp.int32), kv_band, pos_k, rel_bias)

    qb = q.reshape(B, nb, Q_BLOCK, H_NSA, HEAD_DIM).swapaxes(0, 1)
    out = lax.map(block, (qb, jnp.arange(nb, dtype=jnp.int32) * Q_BLOCK))
    return out.swapaxes(0, 1).reshape(B, T, H_NSA, HEAD_DIM)


def combine_nsa(gates, o_cmp, o_sel, o_win):
    return gates[..., 0:1] * o_cmp + gates[..., 1:2] * o_sel + gates[..., 2:3] * o_win


def merge_heads(x, o_sb, o_nsa, gate, g_sb_out, g_nsa_out, w_out, g_post):
    B, T = x.shape[:2]
    o = jnp.concatenate([rmsnorm(o_sb.reshape(B, T, D_SB), g_sb_out),
                         rmsnorm(o_nsa.reshape(B, T, D_NSA), g_nsa_out)], axis=-1)
    return x + gate * rmsnorm(o @ w_out, g_post)


def expert_ffn(xb, w_gu, b_gu, w_dn, b_dn):
    gu = xb @ w_gu + b_gu
    glu = jnp.minimum(gu[:, :D_FF], SWIGLU_LIMIT)
    lin = jnp.clip(gu[:, D_FF:], -SWIGLU_LIMIT, SWIGLU_LIMIT)
    return (glu * jax.nn.sigmoid(SWIGLU_ALPHA * glu) * (lin + 1)) @ w_dn + b_dn


def moe(h, w_router, b_router, w_gu, b_gu, w_dn, b_dn):
    N, D = h.shape
    logits = (h @ w_router + b_router).astype(jnp.float32)
    top_val, top_idx = lax.top_k(logits, TOP_K)
    gate = jax.nn.softmax(top_val, axis=-1).reshape(-1)
    expert = top_idx.reshape(-1)
    token = jnp.repeat(jnp.arange(N, dtype=jnp.int32), TOP_K)
    order = jnp.argsort(expert)
    expert, token, gate = expert[order], token[order], gate[order]
    counts = jax.ops.segment_sum(jnp.ones_like(expert), expert, num_segments=N_EXPERTS)
    padded = (counts + MOE_BLOCK - 1) // MOE_BLOCK * MOE_BLOCK
    rank = jnp.arange(N * TOP_K, dtype=jnp.int32) - (jnp.cumsum(counts) - counts)[expert]
    dest = (jnp.cumsum(padded) - padded)[expert] + rank
    n_blocks = -(-N * TOP_K // MOE_BLOCK) + N_EXPERTS
    x_buf = jnp.zeros((n_blocks * MOE_BLOCK, D), h.dtype).at[dest].set(h[token])
    blk_expert = jnp.minimum(jnp.searchsorted(jnp.cumsum(padded), jnp.arange(n_blocks, dtype=jnp.int32) * MOE_BLOCK,
                                              side='right'), N_EXPERTS - 1)
    y_buf = lax.map(lambda a: expert_ffn(a[0], w_gu[a[1]], b_gu[a[1]], w_dn[a[1]], b_dn[a[1]]),
                    (x_buf.reshape(n_blocks, MOE_BLOCK, D), blk_expert))
    y = y_buf.reshape(n_blocks * MOE_BLOCK, D)[dest]
    return jnp.zeros((N, D), h.dtype).at[token].add(gate[:, None].astype(h.dtype) * y)


def ffn_sublayer(x, shift, scale, gate, g_pre, g_post, w_router, b_router, w_gu, b_gu, w_dn, b_dn):
    B, T, D = x.shape
    h = modulate(x, g_pre, shift, scale).reshape(B * T, D)
    y = moe(h, w_router, b_router, w_gu, b_gu, w_dn, b_dn).reshape(B, T, D)
    return x + gate * rmsnorm(y, g_post)


def setup_inputs(seed: int = 0) -> dict:
    key = jax.random.key(seed)
    ks = jax.random.split(key, 28)
    f32 = jnp.float32

    def nrm(i, shape, scale=1.0):
        return scale * jax.random.normal(ks[i], shape, f32)

    def gain(i, shape):
        return 1.0 + nrm(i, shape, 0.05)

    n_pages = PAST_LEN // PAGE_SIZE
    n_used = DEC_BATCH * n_pages
    n_pool = n_used + max(n_used // 4, 1)
    win_buf = min(WINDOW, PAST_LEN)
    page_table = jax.random.permutation(ks[27], n_pool)[:n_used].reshape(DEC_BATCH, n_pages).astype(jnp.int32)
    return {
        'x_prompt': nrm(0, (BATCH, SEQ, D_MODEL)),
        'x_sample': nrm(1, (DEC_BATCH, DEC_SEQ, D_MODEL)),
        'cache_sb_kv': nrm(2, (DEPTH, n_pool, PAGE_SIZE, 2, H_SB, HEAD_DIM)),
        'cache_cmp_kv': nrm(3, (DEPTH, n_pool, PAGE_SIZE, 2, G_NSA, HEAD_DIM)),
        'cache_sel_kv': nrm(4, (DEPTH, n_pool, PAGE_SIZE, 2, G_NSA, HEAD_DIM)),
        'state_win_kv': nrm(5, (DEPTH, DEC_BATCH, win_buf, 2, G_NSA, HEAD_DIM)),
        'page_table': page_table,
        'c_prompt': nrm(6, (BATCH, D_MODEL)),
        'c_sample': nrm(7, (DEC_BATCH, D_MODEL)),
        'w_ada': nrm(8, (DEPTH, D_MODEL, 6 * D_MODEL), 0.3 * D_MODEL ** -0.5),
        'b_ada': nrm(9, (DEPTH, 6 * D_MODEL), 0.02),
        'g_attn_pre': gain(10, (DEPTH, D_MODEL)),
        'g_attn_post': gain(11, (DEPTH, D_MODEL)),
        'w_in': nrm(12, (DEPTH, D_MODEL, D_IN), D_MODEL ** -0.5),
        'w_cmp_pos': CMP_BLOCK ** -0.5 * (1.0 + nrm(13, (DEPTH, 2, CMP_BLOCK), 0.1)),
        'rel_bias': nrm(14, (N_BUCKETS, H_NSA), 0.1),
        'g_sb_out': gain(15, (DEPTH, D_SB)),
        'g_nsa_out': gain(16, (DEPTH, D_NSA)),
        'w_out': nrm(17, (DEPTH, D_SB + D_NSA, D_MODEL), (D_SB + D_NSA) ** -0.5),
        'g_ffn_pre': gain(18, (DEPTH, D_MODEL)),
        'g_ffn_post': gain(19, (DEPTH, D_MODEL)),
        'w_router': nrm(20, (DEPTH, D_MODEL, N_EXPERTS), D_MODEL ** -0.5),
        'b_router': nrm(21, (DEPTH, N_EXPERTS), 0.01),
        'w_gu': nrm(22, (DEPTH, N_EXPERTS, D_MODEL, 2 * D_FF), D_MODEL ** -0.5),
        'b_gu': nrm(23, (DEPTH, N_EXPERTS, 2 * D_FF), 0.01),
        'w_dn': nrm(24, (DEPTH, N_EXPERTS, D_FF, D_MODEL), D_FF ** -0.5),
        'b_dn': nrm(25, (DEPTH, N_EXPERTS, D_MODEL), 0.01),
    }


def reference(x_prompt, x_sample, cache_sb_kv, cache_cmp_kv, cache_sel_kv, state_win_kv, page_table,
              c_prompt, c_sample, w_ada, b_ada, g_attn_pre, g_attn_post, w_in, w_cmp_pos, rel_bias,
              g_sb_out, g_nsa_out, w_out, g_ffn_pre, g_ffn_post, w_router, b_router, w_gu, b_gu, w_dn, b_dn):
    T, Ts = x_prompt.shape[1], x_sample.shape[1]
    past = page_table.shape[1] * cache_sb_kv.shape[2]
    win_buf = state_win_kv.shape[2]
    pos_s = past + jnp.arange(Ts, dtype=jnp.int32)
    pos_w = past - win_buf + jnp.arange(win_buf + Ts, dtype=jnp.int32)
    xp, xs = x_prompt, x_sample
    sb_p, sb_s, cmp_p, cmp_s, sel_p, sel_s, win_p, win_s = ([] for _ in range(8))
    for l in range(DEPTH):
        ffn_w = (w_router[l], b_router[l], w_gu[l], b_gu[l], w_dn[l], b_dn[l])
        sh_a, sc_a, gt_a, sh_f, sc_f, gt_f = adaln(c_prompt, w_ada[l], b_ada[l])
        sb_q, sb_kv, nsa_q, cmp_kv, sel_kv, win_kv, gates = project(modulate(xp, g_attn_pre[l], sh_a, sc_a), w_in[l])
        o_cmp, o_sel = nsa_prompt(nsa_q, cmp_kv, sel_kv, w_cmp_pos[l], rel_bias)
        o_nsa = combine_nsa(gates, o_cmp, o_sel, window_prompt(nsa_q, win_kv, rel_bias))
        xp = merge_heads(xp, sb_prompt(sb_q, sb_kv), o_nsa, gt_a, g_sb_out[l], g_nsa_out[l], w_out[l], g_attn_post[l])
        xp = ffn_sublayer(xp, sh_f, sc_f, gt_f, g_ffn_pre[l], g_ffn_post[l], *ffn_w)
        sb_p.append(sb_kv)
        cmp_p.append(cmp_kv)
        sel_p.append(sel_kv)
        win_p.append(jnp.pad(win_kv, ((0, 0), (win_buf, 0), (0, 0), (0, 0), (0, 0)))[:, T:])
        sh_a, sc_a, gt_a, sh_f, sc_f, gt_f = adaln(c_sample, w_ada[l], b_ada[l])
        sb_q, sb_kv, nsa_q, cmp_kv, sel_kv, win_kv, gates = project(modulate(xs, g_attn_pre[l], sh_a, sc_a), w_in[l])
        o_cmp, o_sel = nsa_sample(nsa_q, pos_s, cmp_kv, sel_kv, cache_cmp_kv, cache_sel_kv, l, page_table,
                                  w_cmp_pos[l], rel_bias)
        win_rows = jnp.concatenate([state_win_kv[l], win_kv], axis=1)
        o_nsa = combine_nsa(gates, o_cmp, o_sel, window_attend(nsa_q, pos_s, win_rows, pos_w, rel_bias))
        o_sb = sb_sample(sb_q, sb_kv, cache_sb_kv, l, page_table, pos_s)
        xs = merge_heads(xs, o_sb, o_nsa, gt_a, g_sb_out[l], g_nsa_out[l], w_out[l], g_attn_post[l])
        xs = ffn_sublayer(xs, sh_f, sc_f, gt_f, g_ffn_pre[l], g_ffn_post[l], *ffn_w)
        sb_s.append(sb_kv)
        cmp_s.append(cmp_kv)
        sel_s.append(sel_kv)
        win_s.append(win_rows[:, Ts:])
    return (xp, xs, jnp.stack(sb_p), jnp.stack(sb_s), jnp.stack(cmp_p), jnp.stack(cmp_s),
            jnp.stack(sel_p), jnp.stack(sel_s), jnp.stack(win_p), jnp.stack(win_s))
```

```python
import functools
import math

import jax
import jax.numpy as jnp
from jax import lax
from jax.experimental import pallas as pl
from jax.experimental.pallas import tpu as pltpu

F32 = jnp.float32
BF16 = jnp.bfloat16
I32 = jnp.int32
HI = lax.Precision.HIGHEST

HEAD_DIM = 64
H_SB = 8
H_NSA = 8
G_NSA = 2
R_NSA = H_NSA // G_NSA
D_SB = H_SB * HEAD_DIM
D_NSA = H_NSA * HEAD_DIM
D_KV = G_NSA * HEAD_DIM
CMP_BLOCK = 32
SEL_BLOCK = 64
TOP_N = 16
WINDOW = 512
N_BUCKETS = 32
MAX_DISTANCE = 128
N_EXPERTS = 32
TOP_K = 4
SWIGLU_LIMIT = 7.0
SWIGLU_ALPHA = 1.702
EPS = 1e-6
MASK_VALUE = -1e30
SCALE = HEAD_DIM ** -0.5

LANES = 128
QB = 128
GATE_PAD = LANES
VMEM_LIMIT = 56 * 1024 * 1024


def _cparams(sem, vmem=VMEM_LIMIT):
    return pltpu.CompilerParams(dimension_semantics=sem, vmem_limit_bytes=vmem)


def _dot(a, b, precision=None):
    return jnp.dot(a, b, preferred_element_type=F32, precision=precision)


def _dot_nt(a, b, precision=None):
    return lax.dot_general(a, b, (((1,), (1,)), ((), ())), preferred_element_type=F32, precision=precision)


def _iota(shape, axis):
    return lax.broadcasted_iota(I32, shape, axis)


def _rms(x, g):
    return x * lax.rsqrt(jnp.mean(x * x, axis=-1, keepdims=True) + EPS) * g


def _t5_bucket(dist):
    n = jnp.maximum(dist, 0)
    max_exact = N_BUCKETS // 2
    log_ratio = jnp.log(jnp.maximum(n, 1).astype(F32) / max_exact) / math.log(MAX_DISTANCE / max_exact)
    large = jnp.minimum(max_exact + (log_ratio * (N_BUCKETS - max_exact)).astype(I32), N_BUCKETS - 1)
    return jnp.where(n < max_exact, n, large)


def _adaln_kernel(c_ref, w_ref, b_ref, o_ref):
    c = c_ref[...]
    o_ref[...] = _dot(c * jax.nn.sigmoid(c), w_ref[...], HI) + b_ref[...]


def _adaln(c, w, b):
    r, d = c.shape
    n = w.shape[1]
    tn = 1024
    return pl.pallas_call(
        _adaln_kernel, grid=(n // tn,),
        in_specs=[pl.BlockSpec((r, d), lambda j: (0, 0)), pl.BlockSpec((d, tn), lambda j: (0, j)),
                  pl.BlockSpec((1, tn), lambda j: (0, j))],
        out_specs=pl.BlockSpec((r, tn), lambda j: (0, j)),
        out_shape=jax.ShapeDtypeStruct((r, n), F32), compiler_params=_cparams(("arbitrary",)), name="adaln",
    )(c, w, b.reshape(1, n))


def _mod_spec(mod, tm, rows_per_group):
    g, r, d = mod.shape
    tiles = rows_per_group // tm if r == 1 else 1
    return pl.BlockSpec((1, r, d), lambda i: (i // tiles, 0, 0))


_SPLITS = (D_SB, 2 * D_SB, D_NSA, 2 * D_KV, 2 * D_KV, 2 * D_KV, GATE_PAD)


def _proj_kernel(x_ref, g_ref, sh_ref, sc_ref, w_ref, *out_refs):
    h = (_rms(x_ref[...], g_ref[...]) * (1.0 + sc_ref[0]) + sh_ref[0]).astype(BF16)
    off = 0
    for k, (ref, n) in enumerate(zip(out_refs, _SPLITS)):
        r = _dot(h, w_ref[:, off:off + n])
        ref[...] = jax.nn.sigmoid(r) if k == len(_SPLITS) - 1 else r
        off += n


def _proj_in(x, g_pre, shift, scale, w_in_b, rows_per_group, tm):
    n, d = x.shape
    nw = w_in_b.shape[1]
    row = lambda c: pl.BlockSpec((tm, c), lambda i: (i, 0))
    return pl.pallas_call(
        _proj_kernel, grid=(n // tm,),
        in_specs=[row(d), pl.BlockSpec((1, d), lambda i: (0, 0)), _mod_spec(shift, tm, rows_per_group),
                  _mod_spec(scale, tm, rows_per_group), pl.BlockSpec((d, nw), lambda i: (0, 0))],
        out_specs=[row(c) for c in _SPLITS],
        out_shape=[jax.ShapeDtypeStruct((n, c), F32) for c in _SPLITS],
        compiler_params=_cparams(("arbitrary",)), name="proj_in",
    )(x, g_pre.reshape(1, d), shift, scale, w_in_b)


def _sb_terms(z):
    l = jnp.log(1.0 + jnp.exp(-jnp.abs(z)))
    ls = jnp.minimum(z, 0.0) - l
    return ls, ls - z


def _cumsum_rhs(blk):
    r, c = _iota((blk, 2 * blk), 0), _iota((blk, 2 * blk), 1)
    return jnp.where((r > c) | (c >= blk), 1.0, 0.0).astype(BF16)


def _split_dot(x, rhs):
    hi = x.astype(BF16)
    lo = (x - hi.astype(F32)).astype(BF16)
    return _dot(hi, rhs) + _dot(lo, rhs)


def _sb_prompt_kernel(q_ref, k_ref, v_ref, o_ref):
    i = pl.program_id(2)
    lane = _iota((QB, LANES), 1)
    row, col = _iota((QB, QB), 0), _iota((QB, QB), 1)
    q = q_ref[...] * SCALE
    qh = [jnp.where((lane >= HEAD_DIM) == (hf == 1), q, 0.0).astype(BF16) for hf in (0, 1)]
    u2 = _cumsum_rhs(QB)

    def step(j, carry, diag):
        rows = pl.ds(pl.multiple_of(j * QB, QB), QB)
        kb = k_ref[rows, :].astype(BF16)
        vb = v_ref[rows, :].astype(BF16)
        new = []
        for hf in (0, 1):
            acc, cum = carry[2 * hf], carry[2 * hf + 1]
            ls, lk = _sb_terms(_dot_nt(qh[hf], kb))
            if diag:
                lk = jnp.where(col < row, lk, 0.0)
            cs = _split_dot(lk, u2)
            a = jnp.exp(ls + cs[:, :QB] + cum)
            if diag:
                a = jnp.where(col < row, a, 0.0)
            new += [acc + _dot(a.astype(BF16), vb), cum + cs[:, QB:]]
        return tuple(new)

    z0 = jnp.zeros((QB, LANES), F32)
    carry = step(i, (z0, z0, z0, z0), True)
    carry = lax.fori_loop(0, i, lambda jj, c: step(i - 1 - jj, c, False), carry)
    o_ref[...] = jnp.where(lane < HEAD_DIM, carry[0], carry[2])


def _sb_prompt(sb_q, sb_kv, b, t):
    nq = t // QB
    npair = D_SB // LANES
    return pl.pallas_call(
        _sb_prompt_kernel, grid=(b, npair, nq),
        in_specs=[pl.BlockSpec((QB, LANES), lambda bi, p, i: (bi * nq + i, p)),
                  pl.BlockSpec((t, LANES), lambda bi, p, i: (bi, p)),
                  pl.BlockSpec((t, LANES), lambda bi, p, i: (bi, npair + p))],
        out_specs=pl.BlockSpec((QB, LANES), lambda bi, p, i: (bi * nq + i, p)),
        out_shape=jax.ShapeDtypeStruct((b * t, D_SB), F32),
        compiler_params=_cparams(("arbitrary", "arbitrary", "arbitrary")), name="sb_prompt",
    )(sb_q, sb_kv, sb_kv)


def _head_to_group_half(x2, h, lane):
    g = h // R_NSA
    if (h % 2) != g:
        x2 = pltpu.roll(x2, HEAD_DIM, axis=1)
    return jnp.where((lane >= HEAD_DIM) == (g == 1), x2, 0.0)


def _pair_tile(res, p, lane):
    g = (2 * p) // R_NSA
    a = res[2 * p] if g == 0 else pltpu.roll(res[2 * p], HEAD_DIM, axis=1)
    b = res[2 * p + 1] if g == 1 else pltpu.roll(res[2 * p + 1], HEAD_DIM, axis=1)
    return jnp.where(lane < HEAD_DIM, a, b)


def _cmp_prompt_kernel(q_ref, ckv_ref, wt_ref, fc_ref, ocmp_ref, imp_ref, ck_s, cv_s, *, t):
    i = pl.program_id(1)
    n_cmp, n_sel = t // CMP_BLOCK, t // SEL_BLOCK

    @pl.when(i == 0)
    def _():
        onb = (_iota((n_cmp, t), 1) // CMP_BLOCK) == _iota((n_cmp, t), 0)
        ck_s[...] = _dot(jnp.where(onb, wt_ref[0:1, :], 0.0), ckv_ref[:, 0:D_KV], HI)
        cv_s[...] = _dot(jnp.where(onb, wt_ref[1:2, :], 0.0), ckv_ref[:, D_KV:2 * D_KV], HI)

    lane = _iota((QB, LANES), 1)
    tpos = i * QB + _iota((QB, n_cmp), 0)
    valid = tpos >= _iota((QB, n_cmp), 1) * CMP_BLOCK + (CMP_BLOCK - 1)
    ck = ck_s[...]
    cvb = cv_s[...].astype(BF16)
    shift = (4 * i - 4 + n_cmp) % n_cmp
    psum = [jnp.zeros((QB, n_cmp), F32) for _ in range(G_NSA)]
    res = []
    for h in range(H_NSA):
        g, p = h // R_NSA, h // 2
        qh = _head_to_group_half(q_ref[:, p * LANES:(p + 1) * LANES], h, lane)
        s = _dot_nt(qh, ck, HI) * SCALE + pltpu.roll(fc_ref[h], shift, axis=1)
        s = jnp.where(valid, s, MASK_VALUE)
        pr = jnp.where(valid, jnp.exp(s - jnp.max(s, axis=-1, keepdims=True)), 0.0)
        pr = pr / jnp.maximum(jnp.sum(pr, axis=-1, keepdims=True), 1e-30)
        psum[g] = psum[g] + pr
        res.append(_dot(pr.astype(BF16), cvb))
    for p in range(H_NSA // 2):
        ocmp_ref[:, p * LANES:(p + 1) * LANES] = _pair_tile(res, p, lane)
    pr_, pc_ = _iota((n_cmp, 2 * n_sel), 0), _iota((n_cmp, 2 * n_sel), 1)
    per = SEL_BLOCK // CMP_BLOCK
    imp = (_dot(psum[0], jnp.where(pc_ == pr_ // per, 1.0, 0.0), HI)
           + _dot(psum[1], jnp.where(pc_ == n_sel + pr_ // per, 1.0, 0.0), HI))
    tq = i * QB + _iota((QB, 2 * n_sel), 0)
    j = _iota((QB, 2 * n_sel), 1) % n_sel
    imp_ref[...] = jnp.where(tq // SEL_BLOCK == j, jnp.inf, jnp.where(j * SEL_BLOCK > tq, -jnp.inf, imp))


def _cmp_prompt(nsa_q, cmp_kv, wt, fc, b, t):
    nq = t // QB
    n_cmp, n_sel = t // CMP_BLOCK, t // SEL_BLOCK
    return pl.pallas_call(
        functools.partial(_cmp_prompt_kernel, t=t), grid=(b, nq),
        in_specs=[pl.BlockSpec((QB, D_NSA), lambda bi, i: (bi * nq + i, 0)),
                  pl.BlockSpec((t, 2 * D_KV), lambda bi, i: (bi, 0)),
                  pl.BlockSpec((2, t), lambda bi, i: (0, 0)),
                  pl.BlockSpec((H_NSA, QB, n_cmp), lambda bi, i: (0, 0, 0))],
        out_specs=[pl.BlockSpec((QB, D_NSA), lambda bi, i: (bi * nq + i, 0)),
                   pl.BlockSpec((QB, 2 * n_sel), lambda bi, i: (bi * nq + i, 0))],
        out_shape=[jax.ShapeDtypeStruct((b * t, D_NSA), F32), jax.ShapeDtypeStruct((b * t, 2 * n_sel), F32)],
        scratch_shapes=[pltpu.VMEM((n_cmp, D_KV), F32), pltpu.VMEM((n_cmp, D_KV), F32)],
        compiler_params=_cparams(("arbitrary", "arbitrary")), name="nsa_cmp_prompt",
    )(nsa_q, cmp_kv, wt, fc)


def _topn_mask_t(imp_t, n_sel):
    jrow = _iota(imp_t.shape, 0) % n_sel
    rank = jnp.zeros(imp_t.shape, F32)
    w = imp_t.shape[1]
    for ii in range(n_sel):
        a = jnp.concatenate([jnp.broadcast_to(imp_t[ii:ii + 1, :], (n_sel, w)),
                             jnp.broadcast_to(imp_t[n_sel + ii:n_sel + ii + 1, :], (n_sel, w))], axis=0)
        rank = rank + jnp.where(jrow > ii, jnp.where(a >= imp_t, 1.0, 0.0), jnp.where(a > imp_t, 1.0, 0.0))
    return jnp.where(rank < min(TOP_N, n_sel), 1.0, 0.0)


def _flash_init(m_s, l_s, acc_s):
    m_s[...] = jnp.full(m_s.shape, MASK_VALUE, F32)
    l_s[...] = jnp.zeros(l_s.shape, F32)
    acc_s[...] = jnp.zeros(acc_s.shape, F32)


def _flash_update(s, vb, m_s, l_s, acc_s):
    m_old = m_s[...]
    m_new = jnp.maximum(m_old, jnp.max(s, axis=-1, keepdims=True))
    alpha = jnp.exp(m_old - m_new)
    p = jnp.exp(s - m_new)
    l_s[...] = alpha * l_s[...] + jnp.sum(p, axis=-1, keepdims=True)
    acc_s[...] = alpha * acc_s[...] + _dot(p.astype(BF16), vb)
    m_s[...] = m_new


def _flash_result(l_s, acc_s):
    return acc_s[...] / jnp.maximum(l_s[...], 1e-30)


def _gate_expand(gates, br):
    r, c = _iota((GATE_PAD, D_NSA), 0), _iota((GATE_PAD, D_NSA), 1)
    return _dot(gates, jnp.where(r == 3 * (c // HEAD_DIM) + br, 1.0, 0.0), HI)


def _selwin_kernel(q_ref, imp_ref, gates_ref, ocmp_ref, skv_ref, wkv_ref, tz_ref, o_ref, m_s, l_s, acc_s, *, t):
    i = pl.program_id(1)
    n_sel = t // SEL_BLOCK
    lane = _iota((QB, LANES), 1)
    row, col = _iota((QB, QB), 0), _iota((QB, QB), 1)
    sel = _topn_mask_t(imp_ref[...].T, n_sel).T.astype(BF16)
    er, ec = _iota((2 * n_sel, QB), 0), _iota((2 * n_sel, QB), 1) // SEL_BLOCK
    per = QB // SEL_BLOCK
    o_sel, o_win = [], []
    for g in range(G_NSA):
        qg = jnp.concatenate(
            [_head_to_group_half(q_ref[:, (h // 2) * LANES:(h // 2 + 1) * LANES] * SCALE, h, lane).astype(BF16)
             for h in range(g * R_NSA, (g + 1) * R_NSA)], axis=0)

        _flash_init(m_s, l_s, acc_s)

        def sel_body(j, _):
            rows = pl.ds(pl.multiple_of(j * QB, QB), QB)
            kb = skv_ref[rows, 0:D_KV].astype(BF16)
            vb = skv_ref[rows, D_KV:2 * D_KV].astype(BF16)
            picked = _dot(sel, jnp.where(er == g * n_sel + per * j + ec, 1.0, 0.0).astype(BF16)) > 0.5
            off_diag = jnp.where(j < i, 0.0, MASK_VALUE)
            madd = jnp.where(picked, jnp.where(col <= row, 0.0, off_diag), MASK_VALUE)
            s = _dot_nt(qg, kb) + tz_ref[g, jnp.minimum(i - j, 2)] + jnp.concatenate([madd] * R_NSA, axis=0)
            _flash_update(s, vb, m_s, l_s, acc_s)
            return 0

        lax.fori_loop(0, i + 1, sel_body, 0)
        o_sel.append(_flash_result(l_s, acc_s))

        _flash_init(m_s, l_s, acc_s)
        n_back = WINDOW // QB
        for kind in range(n_back + 1):
            @pl.when(i - kind >= 0)
            def _():
                rows = pl.ds(pl.multiple_of((i - kind) * QB, QB), QB)
                kb = wkv_ref[rows, 0:D_KV].astype(BF16)
                vb = wkv_ref[rows, D_KV:2 * D_KV].astype(BF16)
                s = _dot_nt(qg, kb) + tz_ref[g, min(kind, 2)]
                if kind == 0:
                    s = s + jnp.concatenate([jnp.where(col <= row, 0.0, MASK_VALUE)] * R_NSA, axis=0)
                if kind == n_back:
                    s = s + jnp.concatenate([jnp.where(col > row, 0.0, MASK_VALUE)] * R_NSA, axis=0)
                _flash_update(s, vb, m_s, l_s, acc_s)
        o_win.append(_flash_result(l_s, acc_s))

    def heads(o):
        return [o[h // R_NSA][(h % R_NSA) * QB:(h % R_NSA + 1) * QB, :] for h in range(H_NSA)]

    hs, hw = heads(o_sel), heads(o_win)
    gates = gates_ref[...]
    for p in range(H_NSA // 2):
        cs = slice(p * LANES, (p + 1) * LANES)
        o_ref[:, cs] = (_gate_expand(gates, 0)[:, cs] * ocmp_ref[:, cs] + _gate_expand(gates, 1)[:, cs] * _pair_tile(hs, p, lane)
                        + _gate_expand(gates, 2)[:, cs] * _pair_tile(hw, p, lane))


def _selwin_prompt(nsa_q, imp, gates, o_cmp, sel_kv, win_kv, tz, b, t):
    nq = t // QB
    n_sel = t // SEL_BLOCK
    rowspec = lambda c: pl.BlockSpec((QB, c), lambda bi, i: (bi * nq + i, 0))
    seqspec = pl.BlockSpec((t, 2 * D_KV), lambda bi, i: (bi, 0))
    return pl.pallas_call(
        functools.partial(_selwin_kernel, t=t), grid=(b, nq),
        in_specs=[rowspec(D_NSA), rowspec(2 * n_sel), rowspec(GATE_PAD), rowspec(D_NSA), seqspec, seqspec,
                  pl.BlockSpec((G_NSA, 3, R_NSA * QB, QB), lambda bi, i: (0, 0, 0, 0))],
        out_specs=rowspec(D_NSA),
        out_shape=jax.ShapeDtypeStruct((b * t, D_NSA), F32),
        scratch_shapes=[pltpu.VMEM((R_NSA * QB, LANES), F32)] * 3,
        compiler_params=_cparams(("arbitrary", "arbitrary")), name="nsa_selwin_prompt",
    )(nsa_q, imp, gates, o_cmp, sel_kv, win_kv, tz)


def _head_rows(q, ts):
    rows = ts * H_SB
    qs = jnp.concatenate([jnp.broadcast_to(q[t:t + 1], (H_SB, q.shape[1])) for t in range(ts)], axis=0)
    return jnp.where(_iota((rows, q.shape[1]), 1) // HEAD_DIM == _iota((rows, q.shape[1]), 0) % H_SB, qs, 0.0)


def _collapse_rows(x, ts):
    rows = ts * H_SB
    x = jnp.where(_iota((rows, x.shape[1]), 1) // HEAD_DIM == _iota((rows, x.shape[1]), 0) % H_SB, x, 0.0)
    out = [jnp.sum(x[t * H_SB:(t + 1) * H_SB], axis=0, keepdims=True) for t in range(ts)]
    return jnp.concatenate(out + [jnp.zeros((8 - ts, x.shape[1]), F32)], axis=0)


def _fold_q(q, ts):
    fr, fc = _iota((D_NSA, LANES), 0), _iota((D_NSA, LANES), 1)
    fold = jnp.where(fc == ((fr // HEAD_DIM) // R_NSA) * HEAD_DIM + fr % HEAD_DIM, 1.0, 0.0)
    return _dot(_head_rows(q, ts), fold, HI)


def _unfold_rows(o, ts):
    ur, uc = _iota((LANES, D_NSA), 0), _iota((LANES, D_NSA), 1)
    unf = jnp.where(ur % HEAD_DIM == uc % HEAD_DIM, jnp.where(ur // HEAD_DIM == (uc // HEAD_DIM) // R_NSA, 1.0, 0.0), 0.0)
    return _collapse_rows(_dot(o, unf, HI), ts)


def _sb_sample_kernel(pt_ref, q_ref, new_ref, *rest, pp, ts):
    del pt_ref
    caches, o_ref, (cum_s, acc_s) = rest[:pp], rest[pp], rest[pp + 1:]
    c, nc = pl.program_id(1), pl.num_programs(1)
    rows = ts * H_SB
    qs = _head_rows(q_ref[0] * SCALE, ts).astype(BF16)
    u2 = _cumsum_rhs(QB)

    def block(kv, mask):
        kb = kv[:, 0:D_SB].astype(BF16)
        vb = kv[:, D_SB:2 * D_SB].astype(BF16)
        ls, lk = _sb_terms(_dot_nt(qs, kb))
        if mask is not None:
            lk = jnp.where(mask, lk, 0.0)
        cs = _split_dot(lk, u2)
        a = jnp.exp(ls + cs[:, :QB] + cum_s[...])
        if mask is not None:
            a = jnp.where(mask, a, 0.0)
        acc_s[...] = acc_s[...] + _dot(a.astype(BF16), vb)
        cum_s[...] = cum_s[...] + cs[:, QB:]

    @pl.when(c == 0)
    def _():
        cum_s[...] = jnp.zeros(cum_s.shape, F32)
        acc_s[...] = jnp.zeros(acc_s.shape, F32)
        block(new_ref[0], _iota((rows, QB), 1) < _iota((rows, QB), 0) // H_SB)

    for ip in reversed(range(pp)):
        block(caches[ip][0], None)

    @pl.when(c == nc - 1)
    def _():
        o_ref[0] = _collapse_rows(acc_s[...], ts)


def _paged_specs(width, pp, n_chunks, newest_first):
    def spec(ip):
        if newest_first:
            return pl.BlockSpec((1, QB, width), lambda b, c, pt: (pt[b, (n_chunks - 1 - c) * pp + ip], 0, 0))
        return pl.BlockSpec((1, QB, width), lambda b, c, pt: (pt[b, c * pp + ip], 0, 0))
    return [spec(ip) for ip in range(pp)]


def _batch_spec(shape):
    return pl.BlockSpec((1,) + tuple(shape[1:]), lambda b, c, pt: (b,) + (0,) * (len(shape) - 1))


def _const_spec(shape):
    return pl.BlockSpec(tuple(shape), lambda b, c, pt: (0,) * len(shape))


def _sb_sample(page_table, q, new_kv, cache, pp):
    db, ts, _ = q.shape
    n_pages = page_table.shape[1]
    nc = n_pages // pp
    grid_spec = pltpu.PrefetchScalarGridSpec(
        num_scalar_prefetch=1, grid=(db, nc),
        in_specs=[_batch_spec(q.shape), _batch_spec(new_kv.shape)] + _paged_specs(2 * D_SB, pp, nc, True),
        out_specs=_batch_spec((db, 8, D_SB)),
        scratch_shapes=[pltpu.VMEM((ts * H_SB, QB), F32), pltpu.VMEM((ts * H_SB, D_SB), F32)])
    return pl.pallas_call(
        functools.partial(_sb_sample_kernel, pp=pp, ts=ts), grid_spec=grid_spec,
        out_shape=jax.ShapeDtypeStruct((db, 8, D_SB), F32),
        compiler_params=_cparams(("arbitrary", "arbitrary")), name="sb_sample",
    )(page_table, q, new_kv, *([cache] * pp))


def _cmp_sample_kernel(pt_ref, q_ref, new_ref, wrow_ref, bias_ref, *rest, pp, ts, past, n_cmp, n_sel):
    del pt_ref
    caches, (ocmp_ref, sel_ref), (ck_s, cv_s) = rest[:pp], rest[pp:pp + 2], rest[pp + 2:]
    c, nc = pl.program_id(1), pl.num_programs(1)
    ncp, nsp = ck_s.shape[0], sel_ref.shape[2]
    rows = ts * H_NSA
    per_page = QB // CMP_BLOCK

    @pl.when(c == 0)
    def _():
        ck_s[...] = jnp.zeros(ck_s.shape, F32)
        cv_s[...] = jnp.zeros(cv_s.shape, F32)

    nrow, rcol = _iota((ncp, QB), 0), _iota((ncp, QB), 1)

    def add_rows(kv, first_block):
        on = nrow == jnp.minimum(first_block + rcol // CMP_BLOCK, n_cmp)
        ck_s[...] = ck_s[...] + _dot(jnp.where(on, wrow_ref[0:1, :], 0.0), kv[:, 0:D_KV], HI)
        cv_s[...] = cv_s[...] + _dot(jnp.where(on, wrow_ref[1:2, :], 0.0), kv[:, D_KV:2 * D_KV], HI)

    for ip in range(pp):
        add_rows(caches[ip][0], (c * pp + ip) * per_page)

    @pl.when(c == nc - 1)
    def _():
        add_rows(new_ref[0], past // CMP_BLOCK)
        qc = _fold_q(q_ref[0], ts)
        ncol, tq = _iota((rows, ncp), 1), _iota((rows, ncp), 0) // H_NSA
        valid = ncol * CMP_BLOCK + (CMP_BLOCK - 1) <= past + tq
        s = jnp.where(valid, _dot_nt(qc, ck_s[...], HI) * SCALE + bias_ref[...], MASK_VALUE)
        pr = jnp.where(valid, jnp.exp(s - jnp.max(s, axis=-1, keepdims=True)), 0.0)
        pr = pr / jnp.maximum(jnp.sum(pr, axis=-1, keepdims=True), 1e-30)
        ocmp_ref[0] = _unfold_rows(_dot(pr.astype(BF16), cv_s[...].astype(BF16)), ts)
        gr, gc = _iota((QB, rows), 0), _iota((QB, rows), 1)
        gsum = jnp.where(gr < G_NSA * ts, jnp.where(gc == (gr % ts) * H_NSA + (gr // ts) * R_NSA + gc % R_NSA, 1.0, 0.0), 0.0)
        pr_, pc_ = _iota((ncp, nsp), 0), _iota((ncp, nsp), 1)
        imp = _dot(_dot(gsum, pr, HI), jnp.where(pc_ == pr_ // (SEL_BLOCK // CMP_BLOCK), 1.0, 0.0), HI)
        jj, tt = _iota((QB, nsp), 1), past + _iota((QB, nsp), 0) % ts
        imp = jnp.where(tt // SEL_BLOCK == jj, jnp.inf, jnp.where(jj * SEL_BLOCK > tt, -jnp.inf, imp))
        imp_t = imp.T
        ii, j2 = _iota((nsp, nsp), 0), _iota((nsp, nsp), 1)
        picks = []
        for r in range(G_NSA * ts):
            a = jnp.broadcast_to(imp_t[:, r:r + 1], (nsp, nsp))
            bb = jnp.broadcast_to(imp[r:r + 1, :], (nsp, nsp))
            beats = jnp.where(ii < j2, jnp.where(a >= bb, 1.0, 0.0), jnp.where(a > bb, 1.0, 0.0))
            picks.append(jnp.where(jnp.sum(beats, axis=0, keepdims=True) < min(TOP_N, n_sel), 1.0, 0.0))
        sel8 = jnp.concatenate(picks + [jnp.zeros((QB - G_NSA * ts, nsp), F32)], axis=0).astype(BF16)
        er, ec = _iota((rows, QB), 0), _iota((rows, QB), 1)
        expand = jnp.where(ec == ((er % H_NSA) // R_NSA) * ts + er // H_NSA, 1.0, 0.0).astype(BF16)
        sel_ref[0] = _dot(expand, sel8)


def _cmp_sample(page_table, q, new_kv, cache, wrow, bias_c, past, pp):
    db, ts, _ = q.shape
    n_pages = page_table.shape[1]
    nc = n_pages // pp
    total = past + ts + (-(past + ts)) % SEL_BLOCK
    n_cmp, n_sel = total // CMP_BLOCK, total // SEL_BLOCK
    ncp = bias_c.shape[1]
    nsp = -(-n_sel // LANES) * LANES
    grid_spec = pltpu.PrefetchScalarGridSpec(
        num_scalar_prefetch=1, grid=(db, nc),
        in_specs=[_batch_spec(q.shape), _batch_spec(new_kv.shape), _const_spec(wrow.shape), _const_spec(bias_c.shape)]
        + _paged_specs(2 * D_KV, pp, nc, False),
        out_specs=[_batch_spec((db, 8, D_NSA)), _batch_spec((db, ts * H_NSA, nsp))],
        scratch_shapes=[pltpu.VMEM((ncp, D_KV), F32), pltpu.VMEM((ncp, D_KV), F32)])
    return pl.pallas_call(
        functools.partial(_cmp_sample_kernel, pp=pp, ts=ts, past=past, n_cmp=n_cmp, n_sel=n_sel), grid_spec=grid_spec,
        out_shape=[jax.ShapeDtypeStruct((db, 8, D_NSA), F32), jax.ShapeDtypeStruct((db, ts * H_NSA, nsp), F32)],
        compiler_params=_cparams(("arbitrary", "arbitrary")), name="nsa_cmp_sample",
    )(page_table, q, new_kv, wrow, bias_c, *([cache] * pp))


def _selwin_sample_kernel(pt_ref, q_ref, newsel_ref, sel_ref, bias_ref, ocmp_ref, gates_ref, win_ref, newwin_ref,
                          biasw_ref, *rest, pp, ts, n_pages):
    del pt_ref
    caches, o_ref, (qc_s, m_s, l_s, acc_s) = rest[:pp], rest[pp], rest[pp + 1:]
    c, nc = pl.program_id(1), pl.num_programs(1)
    rows = ts * H_NSA
    nsp = sel_ref.shape[2]
    per_page = QB // SEL_BLOCK

    @pl.when(c == 0)
    def _():
        qc_s[...] = _fold_q(q_ref[0] * SCALE, ts).astype(BF16)
        _flash_init(m_s, l_s, acc_s)

    sel = sel_ref[0].astype(BF16)
    er, ec = _iota((nsp, QB), 0), _iota((nsp, QB), 1) // SEL_BLOCK
    kr, tr = _iota((rows, QB), 1), _iota((rows, QB), 0) // H_NSA

    def block(kv, page, allowed):
        picked = _dot(sel, jnp.where(er == per_page * page + ec, 1.0, 0.0).astype(BF16)) > 0.5
        s = _dot_nt(qc_s[...], kv[:, 0:D_KV].astype(BF16)) + bias_ref[page] + jnp.where(picked, allowed, MASK_VALUE)
        _flash_update(s, kv[:, D_KV:2 * D_KV].astype(BF16), m_s, l_s, acc_s)

    for ip in range(pp):
        block(caches[ip][0], c * pp + ip, 0.0)

    @pl.when(c == nc - 1)
    def _():
        new_allowed = jnp.where(kr <= tr, 0.0, MASK_VALUE)
        block(newsel_ref[0], n_pages, new_allowed)
        o_sel = _flash_result(l_s, acc_s)
        _flash_init(m_s, l_s, acc_s)
        w = win_ref.shape[1]
        for wb in range(w // QB):
            kv = win_ref[0, wb * QB:(wb + 1) * QB, :]
            s = _dot_nt(qc_s[...], kv[:, 0:D_KV].astype(BF16)) + biasw_ref[:, wb * QB:(wb + 1) * QB]
            if wb * QB <= w - WINDOW + ts - 1:
                s = s + jnp.where(wb * QB + kr > w - WINDOW + tr, 0.0, MASK_VALUE)
            _flash_update(s, kv[:, D_KV:2 * D_KV].astype(BF16), m_s, l_s, acc_s)
        kv = newwin_ref[0]
        s = _dot_nt(qc_s[...], kv[:, 0:D_KV].astype(BF16)) + biasw_ref[:, w:w + QB] + new_allowed
        _flash_update(s, kv[:, D_KV:2 * D_KV].astype(BF16), m_s, l_s, acc_s)
        o_win = _flash_result(l_s, acc_s)
        gates = jnp.concatenate([gates_ref[0], jnp.zeros((8 - ts, GATE_PAD), F32)], axis=0)
        o_ref[0] = (_gate_expand(gates, 0) * ocmp_ref[0] + _gate_expand(gates, 1) * _unfold_rows(o_sel, ts)
                    + _gate_expand(gates, 2) * _unfold_rows(o_win, ts))


def _selwin_sample(page_table, q, new_sel, sel32, bias_s, o_cmp, gates, state_win, new_win, bias_w, cache, pp):
    db, ts, _ = q.shape
    n_pages = page_table.shape[1]
    nc = n_pages // pp
    rows = ts * H_NSA
    ins = [q, new_sel, sel32, bias_s, o_cmp, gates, state_win, new_win, bias_w]
    specs = [_batch_spec(q.shape), _batch_spec(new_sel.shape), _batch_spec(sel32.shape), _const_spec(bias_s.shape),
             _batch_spec(o_cmp.shape), _batch_spec(gates.shape), _batch_spec(state_win.shape), _batch_spec(new_win.shape),
             _const_spec(bias_w.shape)]
    grid_spec = pltpu.PrefetchScalarGridSpec(
        num_scalar_prefetch=1, grid=(db, nc),
        in_specs=specs + _paged_specs(2 * D_KV, pp, nc, False),
        out_specs=_batch_spec((db, 8, D_NSA)),
        scratch_shapes=[pltpu.VMEM((rows, LANES), BF16)] + [pltpu.VMEM((rows, LANES), F32)] * 3)
    return pl.pallas_call(
        functools.partial(_selwin_sample_kernel, pp=pp, ts=ts, n_pages=n_pages), grid_spec=grid_spec,
        out_shape=jax.ShapeDtypeStruct((db, 8, D_NSA), F32),
        compiler_params=_cparams(("arbitrary", "arbitrary")), name="nsa_selwin_sample",
    )(page_table, *ins, *([cache] * pp))


def _merge_kernel(x_ref, osb_ref, onsa_ref, gt_ref, shf_ref, scf_ref, gsb_ref, gnsa_ref, gpost_ref, gpre_ref,
                  wout_ref, wr_ref, br_ref, xa_ref, hp_ref, lg_ref):
    a = _rms(osb_ref[...], gsb_ref[...]).astype(BF16)
    b = _rms(onsa_ref[...], gnsa_ref[...]).astype(BF16)
    y = _dot(a, wout_ref[0:D_SB, :]) + _dot(b, wout_ref[D_SB:D_SB + D_NSA, :])
    xa = x_ref[...] + gt_ref[0] * _rms(y, gpost_ref[...])
    xa_ref[...] = xa
    h = _rms(xa, gpre_ref[...]) * (1.0 + scf_ref[0]) + shf_ref[0]
    lg_ref[...] = _dot(h, wr_ref[...], HI) + br_ref[...]
    half = h.shape[1] // 2
    hb = pltpu.bitcast(h.astype(BF16).astype(F32), jnp.uint32)
    hp_ref[...] = hb[:, :half] | (hb[:, half:] >> 16)


def _merge(x, o_sb, o_nsa, gt_a, sh_f, sc_f, g_sb, g_nsa, g_post, g_pre, w_out_b, w_r, b_r, rows_per_group, tm):
    n, d = x.shape
    row = lambda c: pl.BlockSpec((tm, c), lambda i: (i, 0))
    full = lambda a: pl.BlockSpec(a.shape, lambda i: (0,) * a.ndim)
    vec = lambda v: v.reshape(1, -1)
    consts = [vec(g_sb), vec(g_nsa), vec(g_post), vec(g_pre), w_out_b, w_r, vec(b_r)]
    return pl.pallas_call(
        _merge_kernel, grid=(n // tm,),
        in_specs=[row(d), row(D_SB), row(D_NSA)] + [_mod_spec(m, tm, rows_per_group) for m in (gt_a, sh_f, sc_f)]
        + [full(a) for a in consts],
        out_specs=[row(d), row(d // 2), row(LANES)],
        out_shape=[jax.ShapeDtypeStruct((n, d), F32), jax.ShapeDtypeStruct((n, d // 2), jnp.uint32),
                   jax.ShapeDtypeStruct((n, LANES), F32)],
        compiler_params=_cparams(("arbitrary",)), name="merge_heads",
    )(x, o_sb, o_nsa, gt_a, sh_f, sc_f, *consts)


def _route_kernel(lg_ref, info_ref, cnt_ref, carry_s, *, tm):
    i = pl.program_id(0)

    @pl.when(i == 0)
    def _():
        carry_s[...] = jnp.zeros(carry_s.shape, F32)

    lane = _iota((tm, LANES), 1)
    l = jnp.where(lane < N_EXPERTS, lg_ref[...], -jnp.inf)
    tops, idxs, ohs = [], [], []
    for _ in range(TOP_K):
        m = jnp.max(l, axis=-1, keepdims=True)
        idx = jnp.min(jnp.where(l == m, lane, LANES), axis=-1, keepdims=True)
        oh = lane == idx
        l = jnp.where(oh, -jnp.inf, l)
        tops.append(m)
        idxs.append(idx)
        ohs.append(oh)
    e = [jnp.exp(v - tops[0]) for v in tops]
    den = e[0] + e[1] + e[2] + e[3]
    cnt = sum(jnp.where(oh, 1.0, 0.0) for oh in ohs)
    ltri = jnp.where(_iota((tm, tm), 0) > _iota((tm, tm), 1), 1.0, 0.0).astype(BF16)
    prefix = _dot(ltri, cnt.astype(BF16)) + carry_s[0:1, :]
    info = jnp.zeros((tm, LANES), F32)
    for k in range(TOP_K):
        rank = jnp.sum(jnp.where(ohs[k], prefix, 0.0), axis=-1, keepdims=True)
        info = jnp.where(lane == k, idxs[k].astype(F32), info)
        info = jnp.where(lane == TOP_K + k, rank, info)
        info = jnp.where(lane == 2 * TOP_K + k, e[k] / den, info)
    info_ref[...] = info
    carry_s[...] = carry_s[...] + jnp.sum(cnt, axis=0, keepdims=True)
    cnt_ref[...] = carry_s[...]


def _route(logits, tm):
    n = logits.shape[0]
    return pl.pallas_call(
        functools.partial(_route_kernel, tm=tm), grid=(n // tm,),
        in_specs=[pl.BlockSpec((tm, LANES), lambda i: (i, 0))],
        out_specs=[pl.BlockSpec((tm, LANES), lambda i: (i, 0)), pl.BlockSpec((8, LANES), lambda i: (0, 0))],
        out_shape=[jax.ShapeDtypeStruct((n, LANES), F32), jax.ShapeDtypeStruct((8, LANES), F32)],
        scratch_shapes=[pltpu.VMEM((8, LANES), F32)],
        compiler_params=_cparams(("arbitrary",)), name="moe_route",
    )(logits)


def _dispatch_kernel(dest_ref, hp_ref, xz_ref, xbuf_ref, sem, *, tm):
    del xz_ref

    def row_copy(t, d):
        return pltpu.make_async_copy(hp_ref.at[pl.ds(t, 1), :], xbuf_ref.at[pl.ds(d, 1), :], sem)

    def start(t, _):
        for k in range(TOP_K):
            row_copy(t, dest_ref[t * TOP_K + k]).start()
        return 0

    def wait(t, _):
        for k in range(TOP_K):
            row_copy(t, dest_ref[t * TOP_K + k]).wait()
        return 0

    lax.fori_loop(0, tm, start, 0)
    lax.fori_loop(0, tm, wait, 0)


def _dispatch(dest, hp, n_rows, tm):
    n, w = hp.shape
    xz = jnp.zeros((n_rows, w), hp.dtype)
    return pl.pallas_call(
        functools.partial(_dispatch_kernel, tm=tm), grid=(n // tm,),
        in_specs=[pl.BlockSpec((tm * TOP_K,), lambda i: (i,), memory_space=pltpu.SMEM),
                  pl.BlockSpec((tm, w), lambda i: (i, 0)), pl.BlockSpec(memory_space=pl.ANY)],
        out_specs=pl.BlockSpec(memory_space=pl.ANY),
        out_shape=jax.ShapeDtypeStruct((n_rows, w), hp.dtype),
        scratch_shapes=[pltpu.SemaphoreType.DMA(())],
        input_output_aliases={2: 0},
        compiler_params=_cparams(("arbitrary",)), name="moe_dispatch",
    )(dest, hp, xz)


def _ffn_kernel(be_ref, x_ref, wgu_ref, bgu_ref, wdn_ref, bdn_ref, y_ref, wgu_s, wdn_s):
    b = pl.program_id(0)
    changed = jnp.logical_or(b == 0, be_ref[b] != be_ref[jnp.maximum(b - 1, 0)])

    @pl.when(changed)
    def _():
        wgu_s[...] = wgu_ref[0].astype(BF16)
        wdn_s[...] = wdn_ref[0].astype(BF16)

    xw = x_ref[...]
    half = xw.shape[1]
    xa = pltpu.bitcast(xw & jnp.uint32(0xFFFF0000), F32).astype(BF16)
    xb = pltpu.bitcast(xw << 16, F32).astype(BF16)
    gu = _dot(xa, wgu_s[0:half, :]) + _dot(xb, wgu_s[half:2 * half, :]) + bgu_ref[0]
    dff = gu.shape[1] // 2
    glu = jnp.minimum(gu[:, :dff], SWIGLU_LIMIT)
    lin = jnp.clip(gu[:, dff:], -SWIGLU_LIMIT, SWIGLU_LIMIT)
    act = glu * jax.nn.sigmoid(SWIGLU_ALPHA * glu) * (lin + 1.0)
    y_ref[...] = _dot(act.astype(BF16), wdn_s[...]) + bdn_ref[0]


def _ffn(blk_expert, xbuf, w_gu, b_gu, w_dn, b_dn, blk):
    n_rows, half = xbuf.shape
    ne, d, dff2 = w_gu.shape
    dff = dff2 // 2
    grid_spec = pltpu.PrefetchScalarGridSpec(
        num_scalar_prefetch=1, grid=(n_rows // blk,),
        in_specs=[pl.BlockSpec((blk, half), lambda b, be: (b, 0)),
                  pl.BlockSpec((1, d, dff2), lambda b, be: (be[b], 0, 0)),
                  pl.BlockSpec((1, 1, dff2), lambda b, be: (be[b], 0, 0)),
                  pl.BlockSpec((1, dff, d), lambda b, be: (be[b], 0, 0)),
                  pl.BlockSpec((1, 1, d), lambda b, be: (be[b], 0, 0))],
        out_specs=pl.BlockSpec((blk, d), lambda b, be: (b, 0)),
        scratch_shapes=[pltpu.VMEM((d, dff2), BF16), pltpu.VMEM((dff, d), BF16)])
    return pl.pallas_call(
        _ffn_kernel, grid_spec=grid_spec, out_shape=jax.ShapeDtypeStruct((n_rows, d), F32),
        compiler_params=_cparams(("arbitrary",)), name="moe_ffn",
    )(blk_expert, xbuf, w_gu, b_gu.reshape(ne, 1, dff2), w_dn, b_dn.reshape(ne, 1, d))


def _combine_kernel(dest_ref, info_ref, xa_ref, gt_ref, gpost_ref, ybuf_ref, o_ref, rows_s, sem, *, tm):
    def row_copy(t, k):
        return pltpu.make_async_copy(ybuf_ref.at[pl.ds(dest_ref[t * TOP_K + k], 1), :], rows_s.at[k, pl.ds(t, 1), :], sem)

    def start(t, _):
        for k in range(TOP_K):
            row_copy(t, k).start()
        return 0

    def wait(t, _):
        for k in range(TOP_K):
            row_copy(t, k).wait()
        return 0

    lax.fori_loop(0, tm, start, 0)
    lax.fori_loop(0, tm, wait, 0)
    info = info_ref[...]
    y = sum(info[:, 2 * TOP_K + k:2 * TOP_K + k + 1] * rows_s[k] for k in range(TOP_K))
    o_ref[...] = xa_ref[...] + gt_ref[0] * _rms(y, gpost_ref[...])


def _combine(dest, info, xa, gt_f, g_post, ybuf, rows_per_group, tm):
    n, d = xa.shape
    row = lambda c: pl.BlockSpec((tm, c), lambda i: (i, 0))
    return pl.pallas_call(
        functools.partial(_combine_kernel, tm=tm), grid=(n // tm,),
        in_specs=[pl.BlockSpec((tm * TOP_K,), lambda i: (i,), memory_space=pltpu.SMEM), row(LANES), row(d),
                  _mod_spec(gt_f, tm, rows_per_group), pl.BlockSpec((1, d), lambda i: (0, 0)),
                  pl.BlockSpec(memory_space=pl.ANY)],
        out_specs=row(d),
        out_shape=jax.ShapeDtypeStruct((n, d), F32),
        scratch_shapes=[pltpu.VMEM((TOP_K, tm, d), F32), pltpu.SemaphoreType.DMA(())],
        compiler_params=_cparams(("arbitrary",)), name="moe_combine",
    )(dest, info, xa, gt_f, g_post.reshape(1, d), ybuf)


def _moe(hp, logits, xa, gt_f, g_post, w_gu, b_gu, w_dn, b_dn, rows_per_group, blk, tm_route, tm_move):
    n = hp.shape[0]
    info, counts = _route(logits, tm_route)
    idx = info[:, 0:TOP_K].astype(I32)
    rank = info[:, TOP_K:2 * TOP_K].astype(I32)
    cnt = counts[0, :N_EXPERTS].astype(I32)
    padded = (cnt + blk - 1) // blk * blk
    ends = jnp.cumsum(padded)
    dest = ((ends - padded)[idx] + rank).reshape(-1)
    n_blocks = -(-n * TOP_K // blk) + N_EXPERTS
    blk_expert = jnp.minimum(jnp.searchsorted(ends, jnp.arange(n_blocks, dtype=I32) * blk, side='right'),
                             N_EXPERTS - 1).astype(I32)
    xbuf = _dispatch(dest, hp, n_blocks * blk, tm_move)
    ybuf = _ffn(blk_expert, xbuf, w_gu, b_gu, w_dn, b_dn, blk)
    return _combine(dest, info, xa, gt_f, g_post, ybuf, rows_per_group, tm_move)


def _bias_tables(rel_bias, n_cmp):
    r = jnp.arange(QB, dtype=I32)
    d = (jnp.arange(3, dtype=I32) * QB)[:, None, None] + r[None, :, None] - r[None, None, :]
    tz = rel_bias[_t5_bucket(d)].transpose(3, 0, 1, 2)
    tz = tz.reshape(G_NSA, R_NSA, 3, QB, QB).transpose(0, 2, 1, 3, 4).reshape(G_NSA, 3, R_NSA * QB, QB)
    j = jnp.arange(n_cmp, dtype=I32)
    dc = r[:, None] - CMP_BLOCK * (j[None, :] - 4) - (CMP_BLOCK - 1)
    fc = jnp.where((j < 8)[None, :, None], rel_bias[_t5_bucket(dc)], rel_bias[N_BUCKETS - 1][None, None, :])
    return tz, fc.transpose(2, 0, 1)


def _prompt_attention(sb_q, sb_kv, nsa_q, cmp_kv, sel_kv, win_kv, gates, w_cmp_pos, rel_bias, b, t):
    n_cmp = t // CMP_BLOCK
    tz, fc = _bias_tables(rel_bias, n_cmp)
    wt = jnp.tile(w_cmp_pos, (1, n_cmp))
    o_sb = _sb_prompt(sb_q, sb_kv, b, t)
    o_cmp, imp = _cmp_prompt(nsa_q, cmp_kv, wt, fc, b, t)
    o_nsa = _selwin_prompt(nsa_q, imp, gates, o_cmp, sel_kv, win_kv, tz, b, t)
    return o_sb, o_nsa


def _prep_weights(w_in, w_out, w_router, b_router, l):
    pad_in = sum(_SPLITS) - w_in.shape[2]
    pad_r = LANES - N_EXPERTS
    return dict(w_in_b=jnp.pad(w_in[l], ((0, 0), (0, pad_in))).astype(BF16), w_out_b=w_out[l].astype(BF16),
                w_r=jnp.pad(w_router[l], ((0, 0), (0, pad_r))), b_r=jnp.pad(b_router[l], (0, pad_r)))


def _tile(n, want):
    while n % want:
        want //= 2
    return want


def _prompt_layer(x_prompt, mods, w, l):
    b, t, d = x_prompt.shape
    n = b * t
    x = x_prompt.reshape(n, d)
    sh_a, sc_a, gt_a, sh_f, sc_f, gt_f = [m.reshape(b, 1, d) for m in mods]
    sb_q, sb_kv, nsa_q, cmp_kv, sel_kv, win_kv, gates = _proj_in(
        x, w['g_attn_pre'][l], sh_a, sc_a, w['w_in_b'], t, _tile(t, 512))
    o_sb, o_nsa = _prompt_attention(sb_q, sb_kv, nsa_q, cmp_kv, sel_kv, win_kv, gates, w['w_cmp_pos'][l],
                                    w['rel_bias'], b, t)
    xa, hp, logits = _merge(x, o_sb, o_nsa, gt_a, sh_f, sc_f, w['g_sb_out'][l], w['g_nsa_out'][l], w['g_attn_post'][l],
                            w['g_ffn_pre'][l], w['w_out_b'], w['w_r'], w['b_r'], t, _tile(t, 256))
    y = _moe(hp, logits, xa, gt_f, w['g_ffn_post'][l], w['w_gu'][l], w['b_gu'][l], w['w_dn'][l], w['b_dn'][l],
             t, 512, _tile(n, 512), _tile(t, 256))
    win_buf = min(WINDOW, t)
    kv5 = lambda a, heads: a.reshape(b, t, 2, heads, HEAD_DIM)
    return (y.reshape(b, t, d), kv5(sb_kv, H_SB), kv5(cmp_kv, G_NSA), kv5(sel_kv, G_NSA),
            kv5(win_kv, G_NSA)[:, t - win_buf:])


def _sample_bias_tables(rel_bias, past, ts, n_pages, ncp, w):
    t = jnp.repeat(jnp.arange(ts, dtype=I32), H_NSA)
    h = jnp.tile(jnp.arange(H_NSA, dtype=I32), ts)
    look = lambda d: rel_bias[_t5_bucket(d), h[:, None]]
    pos = past + t[:, None]
    bias_c = look(pos - (jnp.arange(ncp, dtype=I32)[None, :] * CMP_BLOCK + CMP_BLOCK - 1))
    bias_s = look(pos - jnp.arange((n_pages + 1) * QB, dtype=I32)[None, :])
    bias_s = bias_s.reshape(ts * H_NSA, n_pages + 1, QB).transpose(1, 0, 2)
    bias_w = look(w + t[:, None] - jnp.arange(w + QB, dtype=I32)[None, :])
    return bias_c, bias_s, bias_w


def _sample_layer(x_sample, mods, caches, state_win, page_table, w, l):
    db, ts, d = x_sample.shape
    ns = db * ts
    cache_sb, cache_cmp, cache_sel = caches
    n_pages, page = page_table.shape[1], cache_sb.shape[1]
    past = n_pages * page
    assert page == QB and ts <= 8
    x = x_sample.reshape(ns, d)
    tm = _tile(ns, 256)
    sh_a, sc_a, gt_a, sh_f, sc_f, gt_f = [jnp.repeat(m, ts, axis=0).reshape(ns // tm, tm, d) for m in mods]
    sb_q, sb_kv, nsa_q, cmp_kv, sel_kv, win_kv, gates = _proj_in(x, w['g_attn_pre'][l], sh_a, sc_a, w['w_in_b'], ts, tm)
    per_seq = lambda a: a.reshape(db, ts, a.shape[1])
    new_rows = lambda a: jnp.pad(per_seq(a), ((0, 0), (0, QB - ts), (0, 0)))
    pp = _tile(n_pages, 8)
    o_sb = _sb_sample(page_table, per_seq(sb_q), new_rows(sb_kv), cache_sb.reshape(-1, page, 2 * D_SB), pp)
    total = past + ts + (-(past + ts)) % SEL_BLOCK
    ncp = -(-(total // CMP_BLOCK + 1) // LANES) * LANES
    wbuf = state_win.shape[1]
    bias_c, bias_s, bias_w = _sample_bias_tables(w['rel_bias'], past, ts, n_pages, ncp, wbuf)
    wrow = jnp.tile(w['w_cmp_pos'][l], (1, QB // CMP_BLOCK))
    nq = per_seq(nsa_q)
    o_cmp, sel32 = _cmp_sample(page_table, nq, new_rows(cmp_kv), cache_cmp.reshape(-1, page, 2 * D_KV), wrow, bias_c, past, pp)
    o_nsa = _selwin_sample(page_table, nq, new_rows(sel_kv), sel32, bias_s, o_cmp, per_seq(gates),
                           state_win.reshape(db, wbuf, 2 * D_KV), new_rows(win_kv), bias_w,
                           cache_sel.reshape(-1, page, 2 * D_KV), pp)
    flat = lambda o: o[:, :ts].reshape(ns, o.shape[2])
    xa, hp, logits = _merge(x, flat(o_sb), flat(o_nsa), gt_a, sh_f, sc_f, w['g_sb_out'][l], w['g_nsa_out'][l],
                            w['g_attn_post'][l], w['g_ffn_pre'][l], w['w_out_b'], w['w_r'], w['b_r'], ts, tm)
    y = _moe(hp, logits, xa, gt_f, w['g_ffn_post'][l], w['w_gu'][l], w['b_gu'][l], w['w_dn'][l], w['b_dn'][l],
             ts, 128, _tile(ns, 512), tm)
    kv5 = lambda a, heads: a.reshape(db, ts, 2, heads, HEAD_DIM)
    win_new = jnp.concatenate([state_win, kv5(win_kv, G_NSA)], axis=1)[:, ts:]
    return y.reshape(db, ts, d), kv5(sb_kv, H_SB), kv5(cmp_kv, G_NSA), kv5(sel_kv, G_NSA), win_new


def kernel(x_prompt, x_sample, cache_sb_kv, cache_cmp_kv, cache_sel_kv, state_win_kv, page_table, c_prompt, c_sample, w_ada, b_ada, g_attn_pre, g_attn_post, w_in, w_cmp_pos, rel_bias, g_sb_out, g_nsa_out, w_out, g_ffn_pre, g_ffn_post, w_router, b_router, w_gu, b_gu, w_dn, b_dn):
    depth = w_in.shape[0]
    assert depth == 1
    b = x_prompt.shape[0]
    xp, xs = x_prompt, x_sample
    outs = [[] for _ in range(8)]
    for l in range(depth):
        w = _prep_weights(w_in, w_out, w_router, b_router, l)
        w.update(g_attn_pre=g_attn_pre, g_attn_post=g_attn_post, w_cmp_pos=w_cmp_pos, rel_bias=rel_bias, g_sb_out=g_sb_out,
                 g_nsa_out=g_nsa_out, g_ffn_pre=g_ffn_pre, g_ffn_post=g_ffn_post, w_gu=w_gu, b_gu=b_gu, w_dn=w_dn, b_dn=b_dn)
        mod = _adaln(jnp.concatenate([c_prompt, c_sample], axis=0), w_ada[l], b_ada[l])
        xp, sb_p, cmp_p, sel_p, win_p = _prompt_layer(xp, jnp.split(mod[:b], 6, axis=-1), w, l)
        xs, sb_s, cmp_s, sel_s, win_s = _sample_layer(
            xs, jnp.split(mod[b:], 6, axis=-1), (cache_sb_kv[l], cache_cmp_kv[l], cache_sel_kv[l]), state_win_kv[l],
            page_table, w, l)
        for o, v in zip(outs, (sb_p, sb_s, cmp_p, cmp_s, sel_p, sel_s, win_p, win_s)):
            o.append(v)
    return (xp, xs) + tuple(jnp.stack(o) for o in outs)
```

```python
import functools
import math

import jax
import jax.numpy as jnp
from jax import lax
from jax.experimental import pallas as pl
from jax.experimental.pallas import tpu as pltpu

F32 = jnp.float32
BF16 = jnp.bfloat16
I32 = jnp.int32
HI = lax.Precision.HIGHEST

HEAD_DIM = 64
H_SB = 8
H_NSA = 8
G_NSA = 2
R_NSA = H_NSA // G_NSA
D_SB = H_SB * HEAD_DIM
D_NSA = H_NSA * HEAD_DIM
D_KV = G_NSA * HEAD_DIM
CMP_BLOCK = 32
SEL_BLOCK = 64
TOP_N = 16
WINDOW = 512
N_BUCKETS = 32
MAX_DISTANCE = 128
N_EXPERTS = 32
TOP_K = 4
SWIGLU_LIMIT = 7.0
SWIGLU_ALPHA = 1.702
EPS = 1e-6
MASK_VALUE = -1e30
SCALE = HEAD_DIM ** -0.5

LANES = 128
QB = 128
GATE_PAD = LANES
VMEM_LIMIT = 56 * 1024 * 1024


def _cparams(sem, vmem=VMEM_LIMIT):
    return pltpu.CompilerParams(dimension_semantics=sem, vmem_limit_bytes=vmem)


def _dot(a, b, precision=None):
    return jnp.dot(a, b, preferred_element_type=F32, precision=precision)


def _dot_nt(a, b, precision=None):
    return lax.dot_general(a, b, (((1,), (1,)), ((), ())), preferred_element_type=F32, precision=precision)


def _iota(shape, axis):
    return lax.broadcasted_iota(I32, shape, axis)


def _rms(x, g):
    return x * lax.rsqrt(jnp.mean(x * x, axis=-1, keepdims=True) + EPS) * g


def _t5_bucket(dist):
    n = jnp.maximum(dist, 0)
    max_exact = N_BUCKETS // 2
    log_ratio = jnp.log(jnp.maximum(n, 1).astype(F32) / max_exact) / math.log(MAX_DISTANCE / max_exact)
    large = jnp.minimum(max_exact + (log_ratio * (N_BUCKETS - max_exact)).astype(I32), N_BUCKETS - 1)
    return jnp.where(n < max_exact, n, large)


def _adaln_kernel(c_ref, w_ref, b_ref, o_ref):
    c = c_ref[...]
    o_ref[...] = _dot(c * jax.nn.sigmoid(c), w_ref[...], HI) + b_ref[...]


def _adaln(c, w, b):
    r, d = c.shape
    n = w.shape[1]
    tn = 1024
    return pl.pallas_call(
        _adaln_kernel, grid=(n // tn,),
        in_specs=[pl.BlockSpec((r, d), lambda j: (0, 0)), pl.BlockSpec((d, tn), lambda j: (0, j)),
                  pl.BlockSpec((1, tn), lambda j: (0, j))],
        out_specs=pl.BlockSpec((r, tn), lambda j: (0, j)),
        out_shape=jax.ShapeDtypeStruct((r, n), F32), compiler_params=_cparams(("arbitrary",)), name="adaln",
    )(c, w, b.reshape(1, n))


def _mod_spec(mod, tm, rows_per_group):
    g, r, d = mod.shape
    tiles = rows_per_group // tm if r == 1 else 1
    return pl.BlockSpec((1, r, d), lambda i: (i // tiles, 0, 0))


_SPLITS = (D_SB, 2 * D_SB, D_NSA, 2 * D_KV, 2 * D_KV, 2 * D_KV, GATE_PAD)


def _proj_kernel(x_ref, g_ref, sh_ref, sc_ref, w_ref, *out_refs):
    h = (_rms(x_ref[...], g_ref[...]) * (1.0 + sc_ref[0]) + sh_ref[0]).astype(BF16)
    off = 0
    for k, (ref, n) in enumerate(zip(out_refs, _SPLITS)):
        r = _dot(h, w_ref[:, off:off + n])
        ref[...] = jax.nn.sigmoid(r) if k == len(_SPLITS) - 1 else r
        off += n


def _proj_in(x, g_pre, shift, scale, w_in_b, rows_per_group, tm):
    n, d = x.shape
    nw = w_in_b.shape[1]
    row = lambda c: pl.BlockSpec((tm, c), lambda i: (i, 0))
    return pl.pallas_call(
        _proj_kernel, grid=(n // tm,),
        in_specs=[row(d), pl.BlockSpec((1, d), lambda i: (0, 0)), _mod_spec(shift, tm, rows_per_group),
                  _mod_spec(scale, tm, rows_per_group), pl.BlockSpec((d, nw), lambda i: (0, 0))],
        out_specs=[row(c) for c in _SPLITS],
        out_shape=[jax.ShapeDtypeStruct((n, c), F32) for c in _SPLITS],
        compiler_params=_cparams(("arbitrary",)), name="proj_in",
    )(x, g_pre.reshape(1, d), shift, scale, w_in_b)


def _sb_terms(z):
    l = jnp.log(1.0 + jnp.exp(-jnp.abs(z)))
    ls = jnp.minimum(z, 0.0) - l
    return ls, ls - z


def _cumsum_rhs(blk):
    r, c = _iota((blk, 2 * blk), 0), _iota((blk, 2 * blk), 1)
    return jnp.where((r > c) | (c >= blk), 1.0, 0.0).astype(BF16)


def _split_dot(x, rhs):
    hi = x.astype(BF16)
    lo = (x - hi.astype(F32)).astype(BF16)
    return _dot(hi, rhs) + _dot(lo, rhs)


SB_DEAD = -106.0


def _sb_prompt_kernel(q_ref, k_ref, v_ref, o_ref, acc_s, cum_s):
    i = pl.program_id(1)
    lane = _iota((QB, LANES), 1)
    row, col = _iota((QB, QB), 0), _iota((QB, QB), 1)
    u2 = _cumsum_rhs(QB)
    acc_s[...] = jnp.zeros(acc_s.shape, F32)
    cum_s[...] = jnp.zeros(cum_s.shape, F32)

    def step(j, diag):
        rows = pl.ds(pl.multiple_of(j * QB, QB), QB)
        worst = None
        for p in range(H_SB // 2):
            cols = slice(p * LANES, (p + 1) * LANES)
            kb = k_ref[rows, cols].astype(BF16)
            vb = v_ref[rows, cols].astype(BF16)
            q = q_ref[:, cols] * SCALE
            for hf in (0, 1):
                h = 2 * p + hf
                qh = jnp.where((lane >= HEAD_DIM) == (hf == 1), q, 0.0).astype(BF16)
                ls, lk = _sb_terms(_dot_nt(qh, kb))
                if diag:
                    lk = jnp.where(col < row, lk, 0.0)
                cs = _split_dot(lk, u2)
                a = jnp.exp(ls + cs[:, :QB] + cum_s[h])
                if diag:
                    a = jnp.where(col < row, a, 0.0)
                acc_s[h] = acc_s[h] + _dot(a.astype(BF16), vb)
                cum = cum_s[h] + cs[:, QB:]
                cum_s[h] = cum
                worst = cum if worst is None else jnp.maximum(worst, cum)
        return jnp.max(worst)

    def cond(c):
        return jnp.logical_and(c[0] >= 0, c[1] > SB_DEAD)

    lax.while_loop(cond, lambda c: (c[0] - 1, step(c[0], False)), (i - 1, step(i, True)))
    for p in range(H_SB // 2):
        o_ref[:, p * LANES:(p + 1) * LANES] = jnp.where(lane < HEAD_DIM, acc_s[2 * p], acc_s[2 * p + 1])


def _sb_prompt(sb_q, sb_kv, b, t):
    nq = t // QB
    return pl.pallas_call(
        _sb_prompt_kernel, grid=(b, nq),
        in_specs=[pl.BlockSpec((QB, D_SB), lambda bi, i: (bi * nq + i, 0)),
                  pl.BlockSpec((t, D_SB), lambda bi, i: (bi, 0)),
                  pl.BlockSpec((t, D_SB), lambda bi, i: (bi, 1))],
        out_specs=pl.BlockSpec((QB, D_SB), lambda bi, i: (bi * nq + i, 0)),
        out_shape=jax.ShapeDtypeStruct((b * t, D_SB), F32),
        scratch_shapes=[pltpu.VMEM((H_SB, QB, LANES), F32)] * 2,
        compiler_params=_cparams(("arbitrary", "arbitrary")), name="sb_prompt",
    )(sb_q, sb_kv, sb_kv)


def _head_to_group_half(x2, h, lane):
    g = h // R_NSA
    if (h % 2) != g:
        x2 = pltpu.roll(x2, HEAD_DIM, axis=1)
    return jnp.where((lane >= HEAD_DIM) == (g == 1), x2, 0.0)


def _pair_tile(res, p, lane):
    g = (2 * p) // R_NSA
    a = res[2 * p] if g == 0 else pltpu.roll(res[2 * p], HEAD_DIM, axis=1)
    b = res[2 * p + 1] if g == 1 else pltpu.roll(res[2 * p + 1], HEAD_DIM, axis=1)
    return jnp.where(lane < HEAD_DIM, a, b)


def _cmp_prompt_kernel(q_ref, ckv_ref, wt_ref, fc_ref, ocmp_ref, imp_ref, ck_s, cv_s, *, t):
    i = pl.program_id(1)
    n_cmp, n_sel = t // CMP_BLOCK, t // SEL_BLOCK

    @pl.when(i == 0)
    def _():
        onb = (_iota((n_cmp, t), 1) // CMP_BLOCK) == _iota((n_cmp, t), 0)
        ck_s[...] = _dot(jnp.where(onb, wt_ref[0:1, :], 0.0), ckv_ref[:, 0:D_KV], HI)
        cv_s[...] = _dot(jnp.where(onb, wt_ref[1:2, :], 0.0), ckv_ref[:, D_KV:2 * D_KV], HI)

    lane = _iota((QB, LANES), 1)
    tpos = i * QB + _iota((QB, n_cmp), 0)
    valid = tpos >= _iota((QB, n_cmp), 1) * CMP_BLOCK + (CMP_BLOCK - 1)
    ck = ck_s[...]
    cvb = cv_s[...].astype(BF16)
    shift = (4 * i - 4 + n_cmp) % n_cmp
    psum = [jnp.zeros((QB, n_cmp), F32) for _ in range(G_NSA)]
    res = []
    for h in range(H_NSA):
        g, p = h // R_NSA, h // 2
        qh = _head_to_group_half(q_ref[:, p * LANES:(p + 1) * LANES], h, lane)
        s = _dot_nt(qh, ck, HI) * SCALE + pltpu.roll(fc_ref[h], shift, axis=1)
        s = jnp.where(valid, s, MASK_VALUE)
        pr = jnp.where(valid, jnp.exp(s - jnp.max(s, axis=-1, keepdims=True)), 0.0)
        pr = pr / jnp.maximum(jnp.sum(pr, axis=-1, keepdims=True), 1e-30)
        psum[g] = psum[g] + pr
        res.append(_dot(pr.astype(BF16), cvb))
    for p in range(H_NSA // 2):
        ocmp_ref[:, p * LANES:(p + 1) * LANES] = _pair_tile(res, p, lane)
    pr_, pc_ = _iota((n_cmp, 2 * n_sel), 0), _iota((n_cmp, 2 * n_sel), 1)
    per = SEL_BLOCK // CMP_BLOCK
    imp = (_dot(psum[0], jnp.where(pc_ == pr_ // per, 1.0, 0.0), HI)
           + _dot(psum[1], jnp.where(pc_ == n_sel + pr_ // per, 1.0, 0.0), HI))
    tq = i * QB + _iota((QB, 2 * n_sel), 0)
    j = _iota((QB, 2 * n_sel), 1) % n_sel
    imp_ref[...] = jnp.where(tq // SEL_BLOCK == j, jnp.inf, jnp.where(j * SEL_BLOCK > tq, -jnp.inf, imp))


def _cmp_prompt(nsa_q, cmp_kv, wt, fc, b, t):
    nq = t // QB
    n_cmp, n_sel = t // CMP_BLOCK, t // SEL_BLOCK
    return pl.pallas_call(
        functools.partial(_cmp_prompt_kernel, t=t), grid=(b, nq),
        in_specs=[pl.BlockSpec((QB, D_NSA), lambda bi, i: (bi * nq + i, 0)),
                  pl.BlockSpec((t, 2 * D_KV), lambda bi, i: (bi, 0)),
                  pl.BlockSpec((2, t), lambda bi, i: (0, 0)),
                  pl.BlockSpec((H_NSA, QB, n_cmp), lambda bi, i: (0, 0, 0))],
        out_specs=[pl.BlockSpec((QB, D_NSA), lambda bi, i: (bi * nq + i, 0)),
                   pl.BlockSpec((QB, 2 * n_sel), lambda bi, i: (bi * nq + i, 0))],
        out_shape=[jax.ShapeDtypeStruct((b * t, D_NSA), F32), jax.ShapeDtypeStruct((b * t, 2 * n_sel), F32)],
        scratch_shapes=[pltpu.VMEM((n_cmp, D_KV), F32), pltpu.VMEM((n_cmp, D_KV), F32)],
        compiler_params=_cparams(("arbitrary", "arbitrary")), name="nsa_cmp_prompt",
    )(nsa_q, cmp_kv, wt, fc)


def _topn_mask_t(imp_t, n_sel):
    jrow = _iota(imp_t.shape, 0) % n_sel
    rank = jnp.zeros(imp_t.shape, F32)
    w = imp_t.shape[1]
    for ii in range(n_sel):
        a = jnp.concatenate([jnp.broadcast_to(imp_t[ii:ii + 1, :], (n_sel, w)),
                             jnp.broadcast_to(imp_t[n_sel + ii:n_sel + ii + 1, :], (n_sel, w))], axis=0)
        rank = rank + jnp.where(jrow > ii, jnp.where(a >= imp_t, 1.0, 0.0), jnp.where(a > imp_t, 1.0, 0.0))
    return jnp.where(rank < min(TOP_N, n_sel), 1.0, 0.0)


def _flash_init(m_s, l_s, acc_s):
    m_s[...] = jnp.full(m_s.shape, MASK_VALUE, F32)
    l_s[...] = jnp.zeros(l_s.shape, F32)
    acc_s[...] = jnp.zeros(acc_s.shape, F32)


def _flash_update(s, vb, m_s, l_s, acc_s):
    m_old = m_s[...]
    m_new = jnp.maximum(m_old, jnp.max(s, axis=-1, keepdims=True))
    alpha = jnp.exp(m_old - m_new)
    p = jnp.exp(s - m_new)
    l_s[...] = alpha * l_s[...] + jnp.sum(p, axis=-1, keepdims=True)
    acc_s[...] = alpha * acc_s[...] + _dot(p.astype(BF16), vb)
    m_s[...] = m_new


def _flash_result(l_s, acc_s):
    return acc_s[...] / jnp.maximum(l_s[...], 1e-30)


def _gate_expand(gates, br):
    r, c = _iota((GATE_PAD, D_NSA), 0), _iota((GATE_PAD, D_NSA), 1)
    return _dot(gates, jnp.where(r == 3 * (c // HEAD_DIM) + br, 1.0, 0.0), HI)


SEL_CHUNK = 4 * QB


def _biased_masked(s, madd, tz_ref, g, i, first_block):
    out = []
    for u in range(s.shape[1] // QB):
        cs = slice(u * QB, (u + 1) * QB)
        kind = jnp.clip(i - (first_block + u), 0, 2)
        out.append(s[:, cs] + tz_ref[g, kind] + jnp.concatenate([madd[:, cs]] * R_NSA, axis=0))
    return jnp.concatenate(out, axis=1)


def _selwin_kernel(q_ref, imp_ref, gates_ref, ocmp_ref, skv_ref, wkv_ref, tz_ref, e_ref, o_ref, m_s, l_s, acc_s, mask_s, *, t):
    i = pl.program_id(1)
    n_sel = t // SEL_BLOCK
    lane = _iota((QB, LANES), 1)
    sel = _topn_mask_t(imp_ref[...].T, n_sel).T.astype(BF16)
    n_back = WINDOW // QB
    o_sel, o_win = [], []
    for g in range(G_NSA):
        qg = jnp.concatenate(
            [_head_to_group_half(q_ref[:, (h // 2) * LANES:(h // 2 + 1) * LANES] * SCALE, h, lane).astype(BF16)
             for h in range(g * R_NSA, (g + 1) * R_NSA)], axis=0)

        for c in range(t // SEL_CHUNK):
            cs = slice(c * SEL_CHUNK, (c + 1) * SEL_CHUNK)
            mask_s[:, cs] = jnp.where(_dot(sel, e_ref[g, :, cs]) > 0.5, 0.0, MASK_VALUE)
        _flash_init(m_s, l_s, acc_s)
        qpos = i * QB + _iota((QB, SEL_CHUNK), 0)

        def sel_body(c, _):
            start = pl.multiple_of(c * SEL_CHUNK, SEL_CHUNK)
            kb = skv_ref[pl.ds(start, SEL_CHUNK), 0:D_KV].astype(BF16)
            vb = skv_ref[pl.ds(start, SEL_CHUNK), D_KV:2 * D_KV].astype(BF16)
            causal = jnp.where(start + _iota((QB, SEL_CHUNK), 1) <= qpos, 0.0, MASK_VALUE)
            s = _biased_masked(_dot_nt(qg, kb), mask_s[:, pl.ds(start, SEL_CHUNK)] + causal, tz_ref, g, i,
                               c * (SEL_CHUNK // QB))
            m_old = m_s[...]
            m_new = jnp.maximum(m_old, jnp.max(s, axis=-1, keepdims=True))
            alpha = jnp.exp(m_old - m_new)
            p = jnp.exp(s - jnp.concatenate([m_new] * (SEL_CHUNK // LANES), axis=1))
            l_s[...] = alpha * l_s[...] + jnp.sum(p, axis=-1, keepdims=True)
            acc_s[...] = alpha * acc_s[...] + _dot(p.astype(BF16), vb)
            m_s[...] = m_new
            return 0

        lax.fori_loop(0, (i * QB + QB + SEL_CHUNK - 1) // SEL_CHUNK, sel_body, 0)
        o_sel.append(_flash_result(l_s, acc_s))

        first = jnp.maximum(i - n_back, 0)
        wk = (n_back + 1) * QB
        start = pl.multiple_of(first * QB, QB)
        kb = wkv_ref[pl.ds(start, wk), 0:D_KV].astype(BF16)
        vb = wkv_ref[pl.ds(start, wk), D_KV:2 * D_KV].astype(BF16)
        dist = i * QB + _iota((QB, wk), 0) - (start + _iota((QB, wk), 1))
        inside = jnp.where(dist >= 0, jnp.where(dist < WINDOW, 0.0, MASK_VALUE), MASK_VALUE)
        s = _biased_masked(_dot_nt(qg, kb), inside, tz_ref, g, i, first)
        p = jnp.exp(s - jnp.max(s, axis=-1, keepdims=True))
        o_win.append(_dot(p.astype(BF16), vb) / jnp.maximum(jnp.sum(p, axis=-1, keepdims=True), 1e-30))

    def heads(o):
        return [o[h // R_NSA][(h % R_NSA) * QB:(h % R_NSA + 1) * QB, :] for h in range(H_NSA)]

    hs, hw = heads(o_sel), heads(o_win)
    gates = gates_ref[...]
    for p in range(H_NSA // 2):
        cs = slice(p * LANES, (p + 1) * LANES)
        o_ref[:, cs] = (_gate_expand(gates, 0)[:, cs] * ocmp_ref[:, cs] + _gate_expand(gates, 1)[:, cs] * _pair_tile(hs, p, lane)
                        + _gate_expand(gates, 2)[:, cs] * _pair_tile(hw, p, lane))


def _selwin_prompt(nsa_q, imp, gates, o_cmp, sel_kv, win_kv, tz, b, t):
    nq = t // QB
    n_sel = t // SEL_BLOCK
    assert t % SEL_CHUNK == 0 and t >= WINDOW + QB
    rowspec = lambda c: pl.BlockSpec((QB, c), lambda bi, i: (bi * nq + i, 0))
    seqspec = pl.BlockSpec((t, 2 * D_KV), lambda bi, i: (bi, 0))
    key_block = jnp.arange(t, dtype=I32) // SEL_BLOCK
    expand = (jnp.arange(2 * n_sel, dtype=I32)[None, :, None]
              == jnp.arange(G_NSA, dtype=I32)[:, None, None] * n_sel + key_block[None, None, :]).astype(BF16)
    return pl.pallas_call(
        functools.partial(_selwin_kernel, t=t), grid=(b, nq),
        in_specs=[rowspec(D_NSA), rowspec(2 * n_sel), rowspec(GATE_PAD), rowspec(D_NSA), seqspec, seqspec,
                  pl.BlockSpec((G_NSA, 3, R_NSA * QB, QB), lambda bi, i: (0, 0, 0, 0)),
                  pl.BlockSpec((G_NSA, 2 * n_sel, t), lambda bi, i: (0, 0, 0))],
        out_specs=rowspec(D_NSA),
        out_shape=jax.ShapeDtypeStruct((b * t, D_NSA), F32),
        scratch_shapes=[pltpu.VMEM((R_NSA * QB, LANES), F32)] * 3 + [pltpu.VMEM((QB, t), F32)],
        compiler_params=_cparams(("arbitrary", "arbitrary")), name="nsa_selwin_prompt",
    )(nsa_q, imp, gates, o_cmp, sel_kv, win_kv, tz, expand)


def _head_rows(q, ts):
    rows = ts * H_SB
    qs = jnp.concatenate([jnp.broadcast_to(q[t:t + 1], (H_SB, q.shape[1])) for t in range(ts)], axis=0)
    return jnp.where(_iota((rows, q.shape[1]), 1) // HEAD_DIM == _iota((rows, q.shape[1]), 0) % H_SB, qs, 0.0)


def _collapse_rows(x, ts):
    rows = ts * H_SB
    x = jnp.where(_iota((rows, x.shape[1]), 1) // HEAD_DIM == _iota((rows, x.shape[1]), 0) % H_SB, x, 0.0)
    out = [jnp.sum(x[t * H_SB:(t + 1) * H_SB], axis=0, keepdims=True) for t in range(ts)]
    return jnp.concatenate(out + [jnp.zeros((8 - ts, x.shape[1]), F32)], axis=0)


def _fold_q(q, ts):
    fr, fc = _iota((D_NSA, LANES), 0), _iota((D_NSA, LANES), 1)
    fold = jnp.where(fc == ((fr // HEAD_DIM) // R_NSA) * HEAD_DIM + fr % HEAD_DIM, 1.0, 0.0)
    return _dot(_head_rows(q, ts), fold, HI)


def _unfold_rows(o, ts):
    ur, uc = _iota((LANES, D_NSA), 0), _iota((LANES, D_NSA), 1)
    unf = jnp.where(ur % HEAD_DIM == uc % HEAD_DIM, jnp.where(ur // HEAD_DIM == (uc // HEAD_DIM) // R_NSA, 1.0, 0.0), 0.0)
    return _collapse_rows(_dot(o, unf, HI), ts)


def _sb_sample_kernel(pt_ref, q_ref, new_ref, cache_ref, o_ref, buf, sem, cum_s, acc_s, accp_s, *, ts, n_pages):
    b = pl.program_id(0)
    rows = ts * H_SB
    page_keys = QB * H_SB
    u2 = _cumsum_rhs(QB)

    def page_copy(k, slot):
        return pltpu.make_async_copy(cache_ref.at[pt_ref[b, n_pages - 1 - k]], buf.at[slot], sem.at[slot])

    page_copy(0, 0).start()

    q = q_ref[0] * SCALE
    qs = _head_rows(q, ts)
    new = new_ref[0]
    valid = _iota((rows, QB), 1) < _iota((rows, QB), 0) // H_SB
    ls, lk = _sb_terms(_dot_nt(qs.astype(BF16), new[:, 0:D_SB].astype(BF16)))
    cs = _split_dot(jnp.where(valid, lk, 0.0), u2)
    a = jnp.where(valid, jnp.exp(ls + cs[:, :QB]), 0.0)
    acc_s[...] = _dot(a.astype(BF16), new[:, D_SB:2 * D_SB].astype(BF16))
    accp_s[...] = jnp.zeros(accp_s.shape, F32)
    cum_s[...] = cs[:, QB:]

    fr, fc = _iota((D_SB, HEAD_DIM), 0), _iota((D_SB, HEAD_DIM), 1)
    qrow = _dot(qs, jnp.where(fr % HEAD_DIM == fc, 1.0, 0.0), HI).astype(BF16)
    own = _iota((rows, page_keys), 1) % H_SB == _iota((rows, page_keys), 0) % H_SB
    pr_, pc_ = _iota((page_keys, QB), 0), _iota((page_keys, QB), 1)
    gather = jnp.where(pr_ // H_SB == pc_, 1.0, 0.0).astype(BF16)
    sr_, sc_ = _iota((QB, page_keys), 0), _iota((QB, page_keys), 1)
    spread = jnp.where(sc_ // H_SB == sr_, 1.0, 0.0).astype(BF16)

    def walk(c):
        k, _ = c
        slot = k % 2
        page_copy(k, slot).wait()

        @pl.when(k + 1 < n_pages)
        def _():
            page_copy(k + 1, 1 - slot).start()

        kb = buf[slot, :, 0].reshape(page_keys, HEAD_DIM).astype(BF16)
        vb = buf[slot, :, 1].reshape(page_keys, HEAD_DIM).astype(BF16)
        z = _split_dot(jnp.where(own, _dot_nt(qrow, kb), 0.0), gather)
        ls, lk = _sb_terms(z)
        cs = _split_dot(lk, u2)
        a = jnp.exp(ls + cs[:, :QB] + cum_s[...]).astype(BF16)
        a = jnp.where(own, _dot(a, spread), 0.0).astype(BF16)
        accp_s[...] = accp_s[...] + _dot(a, vb)
        cum = cum_s[...] + cs[:, QB:]
        cum_s[...] = cum
        return k + 1, jnp.max(cum)

    k_end, _ = lax.while_loop(lambda c: jnp.logical_and(c[0] < n_pages, c[1] > SB_DEAD), walk,
                              (jnp.int32(0), jnp.max(cum_s[...])))

    @pl.when(k_end < n_pages)
    def _():
        page_copy(k_end, k_end % 2).wait()

    ur, uc = _iota((HEAD_DIM, D_SB), 0), _iota((HEAD_DIM, D_SB), 1)
    paged = _dot(accp_s[...], jnp.where(uc % HEAD_DIM == ur, 1.0, 0.0), HI)
    o_ref[0] = _collapse_rows(acc_s[...] + paged, ts)


def _paged_specs(width, pp, n_chunks, newest_first):
    def spec(ip):
        if newest_first:
            return pl.BlockSpec((1, QB, width), lambda b, c, pt: (pt[b, (n_chunks - 1 - c) * pp + ip], 0, 0))
        return pl.BlockSpec((1, QB, width), lambda b, c, pt: (pt[b, c * pp + ip], 0, 0))
    return [spec(ip) for ip in range(pp)]


def _batch_spec(shape):
    return pl.BlockSpec((1,) + tuple(shape[1:]), lambda b, c, pt: (b,) + (0,) * (len(shape) - 1))


def _const_spec(shape):
    return pl.BlockSpec(tuple(shape), lambda b, c, pt: (0,) * len(shape))


def _sb_sample(page_table, q, new_kv, cache):
    db, ts, _ = q.shape
    n_pages = page_table.shape[1]
    rows = ts * H_SB
    per_seq = lambda shape: pl.BlockSpec((1,) + tuple(shape[1:]), lambda b, pt: (b,) + (0,) * (len(shape) - 1))
    grid_spec = pltpu.PrefetchScalarGridSpec(
        num_scalar_prefetch=1, grid=(db,),
        in_specs=[per_seq(q.shape), per_seq(new_kv.shape), pl.BlockSpec(memory_space=pl.ANY)],
        out_specs=per_seq((db, 8, D_SB)),
        scratch_shapes=[pltpu.VMEM((2,) + tuple(cache.shape[1:]), F32), pltpu.SemaphoreType.DMA((2,)),
                        pltpu.VMEM((rows, QB), F32), pltpu.VMEM((rows, D_SB), F32), pltpu.VMEM((rows, HEAD_DIM), F32)])
    return pl.pallas_call(
        functools.partial(_sb_sample_kernel, ts=ts, n_pages=n_pages), grid_spec=grid_spec,
        out_shape=jax.ShapeDtypeStruct((db, 8, D_SB), F32),
        compiler_params=_cparams(("arbitrary",)), name="sb_sample",
    )(page_table, q, new_kv, cache)


def _cmp_sample_kernel(pt_ref, q_ref, new_ref, wrow_ref, bias_ref, *rest, pp, ts, past, n_cmp, n_sel):
    del pt_ref
    caches, (ocmp_ref, sel_ref), (ck_s, cv_s) = rest[:pp], rest[pp:pp + 2], rest[pp + 2:]
    c, nc = pl.program_id(1), pl.num_programs(1)
    ncp, nsp = ck_s.shape[0], sel_ref.shape[2]
    rows = ts * H_NSA
    per_page = QB // CMP_BLOCK

    @pl.when(c == 0)
    def _():
        ck_s[...] = jnp.zeros(ck_s.shape, F32)
        cv_s[...] = jnp.zeros(cv_s.shape, F32)

    nrow, rcol = _iota((ncp, QB), 0), _iota((ncp, QB), 1)

    def add_rows(kv, first_block):
        on = nrow == jnp.minimum(first_block + rcol // CMP_BLOCK, n_cmp)
        ck_s[...] = ck_s[...] + _dot(jnp.where(on, wrow_ref[0:1, :], 0.0), kv[:, 0:D_KV], HI)
        cv_s[...] = cv_s[...] + _dot(jnp.where(on, wrow_ref[1:2, :], 0.0), kv[:, D_KV:2 * D_KV], HI)

    orow, ocol = _iota((pp * per_page, QB), 0), _iota((pp * per_page, QB), 1)
    ck_new = jnp.zeros((pp * per_page, D_KV), F32)
    cv_new = jnp.zeros((pp * per_page, D_KV), F32)
    for ip in range(pp):
        on = orow == ip * per_page + ocol // CMP_BLOCK
        kv = caches[ip][0]
        ck_new = ck_new + _dot(jnp.where(on, wrow_ref[0:1, :], 0.0), kv[:, 0:D_KV], HI)
        cv_new = cv_new + _dot(jnp.where(on, wrow_ref[1:2, :], 0.0), kv[:, D_KV:2 * D_KV], HI)
    out_rows = pl.ds(pl.multiple_of(c * (pp * per_page), pp * per_page), pp * per_page)
    ck_s[out_rows, :] = ck_new
    cv_s[out_rows, :] = cv_new

    @pl.when(c == nc - 1)
    def _():
        add_rows(new_ref[0], past // CMP_BLOCK)
        qc = _fold_q(q_ref[0], ts)
        ncol, tq = _iota((rows, ncp), 1), _iota((rows, ncp), 0) // H_NSA
        valid = ncol * CMP_BLOCK + (CMP_BLOCK - 1) <= past + tq
        s = jnp.where(valid, _dot_nt(qc, ck_s[...], HI) * SCALE + bias_ref[...], MASK_VALUE)
        pr = jnp.where(valid, jnp.exp(s - jnp.max(s, axis=-1, keepdims=True)), 0.0)
        pr = pr / jnp.maximum(jnp.sum(pr, axis=-1, keepdims=True), 1e-30)
        ocmp_ref[0] = _unfold_rows(_dot(pr.astype(BF16), cv_s[...].astype(BF16)), ts)
        gr, gc = _iota((QB, rows), 0), _iota((QB, rows), 1)
        gsum = jnp.where(gr < G_NSA * ts, jnp.where(gc == (gr % ts) * H_NSA + (gr // ts) * R_NSA + gc % R_NSA, 1.0, 0.0), 0.0)
        pr_, pc_ = _iota((ncp, nsp), 0), _iota((ncp, nsp), 1)
        imp = _dot(_dot(gsum, pr, HI), jnp.where(pc_ == pr_ // (SEL_BLOCK // CMP_BLOCK), 1.0, 0.0), HI)
        jj, tt = _iota((QB, nsp), 1), past + _iota((QB, nsp), 0) % ts
        imp = jnp.where(tt // SEL_BLOCK == jj, jnp.inf, jnp.where(jj * SEL_BLOCK > tt, -jnp.inf, imp))
        imp_t = imp.T
        ii, j2 = _iota((nsp, nsp), 0), _iota((nsp, nsp), 1)
        picks = []
        for r in range(G_NSA * ts):
            a = jnp.broadcast_to(imp_t[:, r:r + 1], (nsp, nsp))
            bb = jnp.broadcast_to(imp[r:r + 1, :], (nsp, nsp))
            beats = jnp.where(ii < j2, jnp.where(a >= bb, 1.0, 0.0), jnp.where(a > bb, 1.0, 0.0))
            picks.append(jnp.where(jnp.sum(beats, axis=0, keepdims=True) < min(TOP_N, n_sel), 1.0, 0.0))
        sel8 = jnp.concatenate(picks + [jnp.zeros((QB - G_NSA * ts, nsp), F32)], axis=0).astype(BF16)
        er, ec = _iota((rows, QB), 0), _iota((rows, QB), 1)
        expand = jnp.where(ec == ((er % H_NSA) // R_NSA) * ts + er // H_NSA, 1.0, 0.0).astype(BF16)
        sel_ref[0] = _dot(expand, sel8)


def _cmp_sample(page_table, q, new_kv, cache, wrow, bias_c, past, pp):
    db, ts, _ = q.shape
    n_pages = page_table.shape[1]
    nc = n_pages // pp
    total = past + ts + (-(past + ts)) % SEL_BLOCK
    n_cmp, n_sel = total // CMP_BLOCK, total // SEL_BLOCK
    ncp = bias_c.shape[1]
    nsp = -(-n_sel // LANES) * LANES
    grid_spec = pltpu.PrefetchScalarGridSpec(
        num_scalar_prefetch=1, grid=(db, nc),
        in_specs=[_batch_spec(q.shape), _batch_spec(new_kv.shape), _const_spec(wrow.shape), _const_spec(bias_c.shape)]
        + _paged_specs(2 * D_KV, pp, nc, False),
        out_specs=[_batch_spec((db, 8, D_NSA)), _batch_spec((db, ts * H_NSA, nsp))],
        scratch_shapes=[pltpu.VMEM((ncp, D_KV), F32), pltpu.VMEM((ncp, D_KV), F32)])
    return pl.pallas_call(
        functools.partial(_cmp_sample_kernel, pp=pp, ts=ts, past=past, n_cmp=n_cmp, n_sel=n_sel), grid_spec=grid_spec,
        out_shape=[jax.ShapeDtypeStruct((db, 8, D_NSA), F32), jax.ShapeDtypeStruct((db, ts * H_NSA, nsp), F32)],
        compiler_params=_cparams(("arbitrary", "arbitrary")), name="nsa_cmp_sample",
    )(page_table, q, new_kv, wrow, bias_c, *([cache] * pp))


def _selwin_sample_kernel(pt_ref, q_ref, newsel_ref, sel_ref, bias_ref, ocmp_ref, gates_ref, win_ref, newwin_ref,
                          biasw_ref, *rest, pp, ts, n_pages):
    del pt_ref
    caches, o_ref, (qc_s, m_s, l_s, acc_s) = rest[:pp], rest[pp], rest[pp + 1:]
    c, nc = pl.program_id(1), pl.num_programs(1)
    rows = ts * H_NSA
    nsp = sel_ref.shape[2]
    per_page = QB // SEL_BLOCK

    @pl.when(c == 0)
    def _():
        qc_s[...] = _fold_q(q_ref[0] * SCALE, ts).astype(BF16)
        _flash_init(m_s, l_s, acc_s)

    sel = sel_ref[0].astype(BF16)
    er, ec = _iota((nsp, QB), 0), _iota((nsp, QB), 1) // SEL_BLOCK
    kr, tr = _iota((rows, QB), 1), _iota((rows, QB), 0) // H_NSA

    def block(kv, page, allowed):
        picked = _dot(sel, jnp.where(er == per_page * page + ec, 1.0, 0.0).astype(BF16)) > 0.5
        s = _dot_nt(qc_s[...], kv[:, 0:D_KV].astype(BF16)) + bias_ref[page] + jnp.where(picked, allowed, MASK_VALUE)
        _flash_update(s, kv[:, D_KV:2 * D_KV].astype(BF16), m_s, l_s, acc_s)

    for ip in range(pp):
        block(caches[ip][0], c * pp + ip, 0.0)

    @pl.when(c == nc - 1)
    def _():
        new_allowed = jnp.where(kr <= tr, 0.0, MASK_VALUE)
        block(newsel_ref[0], n_pages, new_allowed)
        o_sel = _flash_result(l_s, acc_s)
        _flash_init(m_s, l_s, acc_s)
        w = win_ref.shape[1]
        for wb in range(w // QB):
            kv = win_ref[0, wb * QB:(wb + 1) * QB, :]
            s = _dot_nt(qc_s[...], kv[:, 0:D_KV].astype(BF16)) + biasw_ref[:, wb * QB:(wb + 1) * QB]
            if wb * QB <= w - WINDOW + ts - 1:
                s = s + jnp.where(wb * QB + kr > w - WINDOW + tr, 0.0, MASK_VALUE)
            _flash_update(s, kv[:, D_KV:2 * D_KV].astype(BF16), m_s, l_s, acc_s)
        kv = newwin_ref[0]
        s = _dot_nt(qc_s[...], kv[:, 0:D_KV].astype(BF16)) + biasw_ref[:, w:w + QB] + new_allowed
        _flash_update(s, kv[:, D_KV:2 * D_KV].astype(BF16), m_s, l_s, acc_s)
        o_win = _flash_result(l_s, acc_s)
        gates = jnp.concatenate([gates_ref[0], jnp.zeros((8 - ts, GATE_PAD), F32)], axis=0)
        o_ref[0] = (_gate_expand(gates, 0) * ocmp_ref[0] + _gate_expand(gates, 1) * _unfold_rows(o_sel, ts)
                    + _gate_expand(gates, 2) * _unfold_rows(o_win, ts))


def _selwin_sample(page_table, q, new_sel, sel32, bias_s, o_cmp, gates, state_win, new_win, bias_w, cache, pp):
    db, ts, _ = q.shape
    n_pages = page_table.shape[1]
    nc = n_pages // pp
    rows = ts * H_NSA
    ins = [q, new_sel, sel32, bias_s, o_cmp, gates, state_win, new_win, bias_w]
    specs = [_batch_spec(q.shape), _batch_spec(new_sel.shape), _batch_spec(sel32.shape), _const_spec(bias_s.shape),
             _batch_spec(o_cmp.shape), _batch_spec(gates.shape), _batch_spec(state_win.shape), _batch_spec(new_win.shape),
             _const_spec(bias_w.shape)]
    grid_spec = pltpu.PrefetchScalarGridSpec(
        num_scalar_prefetch=1, grid=(db, nc),
        in_specs=specs + _paged_specs(2 * D_KV, pp, nc, False),
        out_specs=_batch_spec((db, 8, D_NSA)),
        scratch_shapes=[pltpu.VMEM((rows, LANES), BF16)] + [pltpu.VMEM((rows, LANES), F32)] * 3)
    return pl.pallas_call(
        functools.partial(_selwin_sample_kernel, pp=pp, ts=ts, n_pages=n_pages), grid_spec=grid_spec,
        out_shape=jax.ShapeDtypeStruct((db, 8, D_NSA), F32),
        compiler_params=_cparams(("arbitrary", "arbitrary")), name="nsa_selwin_sample",
    )(page_table, *ins, *([cache] * pp))


def _merge_kernel(x_ref, osb_ref, onsa_ref, gt_ref, shf_ref, scf_ref, gsb_ref, gnsa_ref, gpost_ref, gpre_ref,
                  wout_ref, wr_ref, br_ref, xa_ref, hp_ref, lg_ref):
    a = _rms(osb_ref[...], gsb_ref[...]).astype(BF16)
    b = _rms(onsa_ref[...], gnsa_ref[...]).astype(BF16)
    y = _dot(a, wout_ref[0:D_SB, :]) + _dot(b, wout_ref[D_SB:D_SB + D_NSA, :])
    xa = x_ref[...] + gt_ref[0] * _rms(y, gpost_ref[...])
    xa_ref[...] = xa
    h = _rms(xa, gpre_ref[...]) * (1.0 + scf_ref[0]) + shf_ref[0]
    lg_ref[...] = _dot(h, wr_ref[...], HI) + br_ref[...]
    half = h.shape[1] // 2
    hb = pltpu.bitcast(h.astype(BF16).astype(F32), jnp.uint32)
    hp_ref[...] = hb[:, :half] | (hb[:, half:] >> 16)


def _merge(x, o_sb, o_nsa, gt_a, sh_f, sc_f, g_sb, g_nsa, g_post, g_pre, w_out_b, w_r, b_r, rows_per_group, tm):
    n, d = x.shape
    row = lambda c: pl.BlockSpec((tm, c), lambda i: (i, 0))
    full = lambda a: pl.BlockSpec(a.shape, lambda i: (0,) * a.ndim)
    vec = lambda v: v.reshape(1, -1)
    consts = [vec(g_sb), vec(g_nsa), vec(g_post), vec(g_pre), w_out_b, w_r, vec(b_r)]
    return pl.pallas_call(
        _merge_kernel, grid=(n // tm,),
        in_specs=[row(d), row(D_SB), row(D_NSA)] + [_mod_spec(m, tm, rows_per_group) for m in (gt_a, sh_f, sc_f)]
        + [full(a) for a in consts],
        out_specs=[row(d), row(d // 2), row(LANES)],
        out_shape=[jax.ShapeDtypeStruct((n, d), F32), jax.ShapeDtypeStruct((n, d // 2), jnp.uint32),
                   jax.ShapeDtypeStruct((n, LANES), F32)],
        compiler_params=_cparams(("arbitrary",)), name="merge_heads",
    )(x, o_sb, o_nsa, gt_a, sh_f, sc_f, *consts)


def _route_kernel(lg_ref, info_ref, cnt_ref, carry_s, *, tm):
    i = pl.program_id(0)

    @pl.when(i == 0)
    def _():
        carry_s[...] = jnp.zeros(carry_s.shape, F32)

    lane = _iota((tm, LANES), 1)
    l = jnp.where(lane < N_EXPERTS, lg_ref[...], -jnp.inf)
    tops, idxs, ohs = [], [], []
    for _ in range(TOP_K):
        m = jnp.max(l, axis=-1, keepdims=True)
        idx = jnp.min(jnp.where(l == m, lane, LANES), axis=-1, keepdims=True)
        oh = lane == idx
        l = jnp.where(oh, -jnp.inf, l)
        tops.append(m)
        idxs.append(idx)
        ohs.append(oh)
    e = [jnp.exp(v - tops[0]) for v in tops]
    den = e[0] + e[1] + e[2] + e[3]
    cnt = sum(jnp.where(oh, 1.0, 0.0) for oh in ohs)
    ltri = jnp.where(_iota((tm, tm), 0) > _iota((tm, tm), 1), 1.0, 0.0).astype(BF16)
    prefix = _dot(ltri, cnt.astype(BF16)) + carry_s[0:1, :]
    info = jnp.zeros((tm, LANES), F32)
    for k in range(TOP_K):
        rank = jnp.sum(jnp.where(ohs[k], prefix, 0.0), axis=-1, keepdims=True)
        info = jnp.where(lane == k, idxs[k].astype(F32), info)
        info = jnp.where(lane == TOP_K + k, rank, info)
        info = jnp.where(lane == 2 * TOP_K + k, e[k] / den, info)
    info_ref[...] = info
    carry_s[...] = carry_s[...] + jnp.sum(cnt, axis=0, keepdims=True)
    cnt_ref[...] = carry_s[...]


def _route(logits, tm):
    n = logits.shape[0]
    return pl.pallas_call(
        functools.partial(_route_kernel, tm=tm), grid=(n // tm,),
        in_specs=[pl.BlockSpec((tm, LANES), lambda i: (i, 0))],
        out_specs=[pl.BlockSpec((tm, LANES), lambda i: (i, 0)), pl.BlockSpec((8, LANES), lambda i: (0, 0))],
        out_shape=[jax.ShapeDtypeStruct((n, LANES), F32), jax.ShapeDtypeStruct((8, LANES), F32)],
        scratch_shapes=[pltpu.VMEM((8, LANES), F32)],
        compiler_params=_cparams(("arbitrary",)), name="moe_route",
    )(logits)


def _dispatch_kernel(dest_ref, hp_ref, xz_ref, xbuf_ref, sem, *, tm):
    del xz_ref

    def row_copy(t, d):
        return pltpu.make_async_copy(hp_ref.at[pl.ds(t, 1), :], xbuf_ref.at[pl.ds(d, 1), :], sem)

    def start(t, _):
        for k in range(TOP_K):
            row_copy(t, dest_ref[t * TOP_K + k]).start()
        return 0

    def wait(t, _):
        for k in range(TOP_K):
            row_copy(t, dest_ref[t * TOP_K + k]).wait()
        return 0

    lax.fori_loop(0, tm, start, 0)
    lax.fori_loop(0, tm, wait, 0)


def _dispatch(dest, hp, n_rows, tm):
    n, w = hp.shape
    xz = jnp.zeros((n_rows, w), hp.dtype)
    return pl.pallas_call(
        functools.partial(_dispatch_kernel, tm=tm), grid=(n // tm,),
        in_specs=[pl.BlockSpec((tm * TOP_K,), lambda i: (i,), memory_space=pltpu.SMEM),
                  pl.BlockSpec((tm, w), lambda i: (i, 0)), pl.BlockSpec(memory_space=pl.ANY)],
        out_specs=pl.BlockSpec(memory_space=pl.ANY),
        out_shape=jax.ShapeDtypeStruct((n_rows, w), hp.dtype),
        scratch_shapes=[pltpu.SemaphoreType.DMA(())],
        input_output_aliases={2: 0},
        compiler_params=_cparams(("arbitrary",)), name="moe_dispatch",
    )(dest, hp, xz)


def _ffn_kernel(be_ref, x_ref, wgu_ref, bgu_ref, wdn_ref, bdn_ref, y_ref, wgu_s, wdn_s):
    b = pl.program_id(0)
    changed = jnp.logical_or(b == 0, be_ref[b] != be_ref[jnp.maximum(b - 1, 0)])

    @pl.when(changed)
    def _():
        wgu_s[...] = wgu_ref[0].astype(BF16)
        wdn_s[...] = wdn_ref[0].astype(BF16)

    xw = x_ref[...]
    half = xw.shape[1]
    xa = pltpu.bitcast(xw & jnp.uint32(0xFFFF0000), F32).astype(BF16)
    xb = pltpu.bitcast(xw << 16, F32).astype(BF16)
    gu = _dot(xa, wgu_s[0:half, :]) + _dot(xb, wgu_s[half:2 * half, :]) + bgu_ref[0]
    dff = gu.shape[1] // 2
    glu = jnp.minimum(gu[:, :dff], SWIGLU_LIMIT)
    lin = jnp.clip(gu[:, dff:], -SWIGLU_LIMIT, SWIGLU_LIMIT)
    act = glu * jax.nn.sigmoid(SWIGLU_ALPHA * glu) * (lin + 1.0)
    y_ref[...] = _dot(act.astype(BF16), wdn_s[...]) + bdn_ref[0]


def _ffn(blk_expert, xbuf, w_gu, b_gu, w_dn, b_dn, blk):
    n_rows, half = xbuf.shape
    ne, d, dff2 = w_gu.shape
    dff = dff2 // 2
    grid_spec = pltpu.PrefetchScalarGridSpec(
        num_scalar_prefetch=1, grid=(n_rows // blk,),
        in_specs=[pl.BlockSpec((blk, half), lambda b, be: (b, 0)),
                  pl.BlockSpec((1, d, dff2), lambda b, be: (be[b], 0, 0)),
                  pl.BlockSpec((1, 1, dff2), lambda b, be: (be[b], 0, 0)),
                  pl.BlockSpec((1, dff, d), lambda b, be: (be[b], 0, 0)),
                  pl.BlockSpec((1, 1, d), lambda b, be: (be[b], 0, 0))],
        out_specs=pl.BlockSpec((blk, d), lambda b, be: (b, 0)),
        scratch_shapes=[pltpu.VMEM((d, dff2), BF16), pltpu.VMEM((dff, d), BF16)])
    return pl.pallas_call(
        _ffn_kernel, grid_spec=grid_spec, out_shape=jax.ShapeDtypeStruct((n_rows, d), F32),
        compiler_params=_cparams(("arbitrary",)), name="moe_ffn",
    )(blk_expert, xbuf, w_gu, b_gu.reshape(ne, 1, dff2), w_dn, b_dn.reshape(ne, 1, d))


def _combine_kernel(dest_ref, info_ref, xa_ref, gt_ref, gpost_ref, ybuf_ref, o_ref, rows_s, sem, *, tm):
    def row_copy(t, k):
        return pltpu.make_async_copy(ybuf_ref.at[pl.ds(dest_ref[t * TOP_K + k], 1), :], rows_s.at[k, pl.ds(t, 1), :], sem)

    def start(t, _):
        for k in range(TOP_K):
            row_copy(t, k).start()
        return 0

    def wait(t, _):
        for k in range(TOP_K):
            row_copy(t, k).wait()
        return 0

    lax.fori_loop(0, tm, start, 0)
    lax.fori_loop(0, tm, wait, 0)
    info = info_ref[...]
    y = sum(info[:, 2 * TOP_K + k:2 * TOP_K + k + 1] * rows_s[k] for k in range(TOP_K))
    o_ref[...] = xa_ref[...] + gt_ref[0] * _rms(y, gpost_ref[...])


def _combine(dest, info, xa, gt_f, g_post, ybuf, rows_per_group, tm):
    n, d = xa.shape
    row = lambda c: pl.BlockSpec((tm, c), lambda i: (i, 0))
    return pl.pallas_call(
        functools.partial(_combine_kernel, tm=tm), grid=(n // tm,),
        in_specs=[pl.BlockSpec((tm * TOP_K,), lambda i: (i,), memory_space=pltpu.SMEM), row(LANES), row(d),
                  _mod_spec(gt_f, tm, rows_per_group), pl.BlockSpec((1, d), lambda i: (0, 0)),
                  pl.BlockSpec(memory_space=pl.ANY)],
        out_specs=row(d),
        out_shape=jax.ShapeDtypeStruct((n, d), F32),
        scratch_shapes=[pltpu.VMEM((TOP_K, tm, d), F32), pltpu.SemaphoreType.DMA(())],
        compiler_params=_cparams(("arbitrary",)), name="moe_combine",
    )(dest, info, xa, gt_f, g_post.reshape(1, d), ybuf)


def _moe(hp, logits, xa, gt_f, g_post, w_gu, b_gu, w_dn, b_dn, rows_per_group, blk, tm_route, tm_move):
    n = hp.shape[0]
    info, counts = _route(logits, tm_route)
    idx = info[:, 0:TOP_K].astype(I32)
    rank = info[:, TOP_K:2 * TOP_K].astype(I32)
    cnt = counts[0, :N_EXPERTS].astype(I32)
    padded = (cnt + blk - 1) // blk * blk
    ends = jnp.cumsum(padded)
    experts = jnp.arange(N_EXPERTS, dtype=I32)
    first_row = jnp.sum(jnp.where(idx[..., None] == experts, ends - padded, 0), axis=-1)
    dest = (first_row + rank).reshape(-1)
    n_blocks = -(-n * TOP_K // blk) + N_EXPERTS
    block_row = jnp.arange(n_blocks, dtype=I32) * blk
    blk_expert = jnp.minimum(jnp.sum((ends[None, :] <= block_row[:, None]).astype(I32), axis=-1), N_EXPERTS - 1)
    xbuf = _dispatch(dest, hp, n_blocks * blk, tm_move)
    ybuf = _ffn(blk_expert, xbuf, w_gu, b_gu, w_dn, b_dn, blk)
    return _combine(dest, info, xa, gt_f, g_post, ybuf, rows_per_group, tm_move)


def _bias_tables(rel_bias, n_cmp):
    r = jnp.arange(QB, dtype=I32)
    d = (jnp.arange(3, dtype=I32) * QB)[:, None, None] + r[None, :, None] - r[None, None, :]
    tz = rel_bias[_t5_bucket(d)].transpose(3, 0, 1, 2)
    tz = tz.reshape(G_NSA, R_NSA, 3, QB, QB).transpose(0, 2, 1, 3, 4).reshape(G_NSA, 3, R_NSA * QB, QB)
    j = jnp.arange(n_cmp, dtype=I32)
    dc = r[:, None] - CMP_BLOCK * (j[None, :] - 4) - (CMP_BLOCK - 1)
    fc = jnp.where((j < 8)[None, :, None], rel_bias[_t5_bucket(dc)], rel_bias[N_BUCKETS - 1][None, None, :])
    return tz, fc.transpose(2, 0, 1)


def _prompt_attention(sb_q, sb_kv, nsa_q, cmp_kv, sel_kv, win_kv, gates, w_cmp_pos, rel_bias, b, t):
    n_cmp = t // CMP_BLOCK
    tz, fc = _bias_tables(rel_bias, n_cmp)
    wt = jnp.tile(w_cmp_pos, (1, n_cmp))
    o_sb = _sb_prompt(sb_q, sb_kv, b, t)
    o_cmp, imp = _cmp_prompt(nsa_q, cmp_kv, wt, fc, b, t)
    o_nsa = _selwin_prompt(nsa_q, imp, gates, o_cmp, sel_kv, win_kv, tz, b, t)
    return o_sb, o_nsa


def _prep_weights(w_in, w_out, w_router, b_router, l):
    pad_in = sum(_SPLITS) - w_in.shape[2]
    pad_r = LANES - N_EXPERTS
    return dict(w_in_b=jnp.pad(w_in[l], ((0, 0), (0, pad_in))).astype(BF16), w_out_b=w_out[l].astype(BF16),
                w_r=jnp.pad(w_router[l], ((0, 0), (0, pad_r))), b_r=jnp.pad(b_router[l], (0, pad_r)))


def _tile(n, want):
    while n % want:
        want //= 2
    return want


def _prompt_layer(x_prompt, mods, w, l):
    b, t, d = x_prompt.shape
    n = b * t
    x = x_prompt.reshape(n, d)
    sh_a, sc_a, gt_a, sh_f, sc_f, gt_f = [m.reshape(b, 1, d) for m in mods]
    sb_q, sb_kv, nsa_q, cmp_kv, sel_kv, win_kv, gates = _proj_in(
        x, w['g_attn_pre'][l], sh_a, sc_a, w['w_in_b'], t, _tile(t, 512))
    o_sb, o_nsa = _prompt_attention(sb_q, sb_kv, nsa_q, cmp_kv, sel_kv, win_kv, gates, w['w_cmp_pos'][l],
                                    w['rel_bias'], b, t)
    xa, hp, logits = _merge(x, o_sb, o_nsa, gt_a, sh_f, sc_f, w['g_sb_out'][l], w['g_nsa_out'][l], w['g_attn_post'][l],
                            w['g_ffn_pre'][l], w['w_out_b'], w['w_r'], w['b_r'], t, _tile(t, 256))
    y = _moe(hp, logits, xa, gt_f, w['g_ffn_post'][l], w['w_gu'][l], w['b_gu'][l], w['w_dn'][l], w['b_dn'][l],
             t, 512, _tile(n, 512), _tile(t, 256))
    win_buf = min(WINDOW, t)
    kv5 = lambda a, heads: a.reshape(b, t, 2, heads, HEAD_DIM)
    return (y.reshape(b, t, d), kv5(sb_kv, H_SB), kv5(cmp_kv, G_NSA), kv5(sel_kv, G_NSA),
            kv5(win_kv, G_NSA)[:, t - win_buf:])


def _sample_bias_tables(rel_bias, past, ts, n_pages, ncp, w):
    t = jnp.repeat(jnp.arange(ts, dtype=I32), H_NSA)
    h = jnp.tile(jnp.arange(H_NSA, dtype=I32), ts)
    look = lambda d: rel_bias[_t5_bucket(d), h[:, None]]
    pos = past + t[:, None]
    bias_c = look(pos - (jnp.arange(ncp, dtype=I32)[None, :] * CMP_BLOCK + CMP_BLOCK - 1))
    bias_s = look(pos - jnp.arange((n_pages + 1) * QB, dtype=I32)[None, :])
    bias_s = bias_s.reshape(ts * H_NSA, n_pages + 1, QB).transpose(1, 0, 2)
    bias_w = look(w + t[:, None] - jnp.arange(w + QB, dtype=I32)[None, :])
    return bias_c, bias_s, bias_w


def _sample_layer(x_sample, mods, caches, state_win, page_table, w, l):
    db, ts, d = x_sample.shape
    ns = db * ts
    cache_sb, cache_cmp, cache_sel = caches
    n_pages, page = page_table.shape[1], cache_sb.shape[1]
    past = n_pages * page
    assert page == QB and ts <= 8
    x = x_sample.reshape(ns, d)
    tm = _tile(ns, 256)
    sh_a, sc_a, gt_a, sh_f, sc_f, gt_f = [jnp.repeat(m, ts, axis=0).reshape(ns // tm, tm, d) for m in mods]
    sb_q, sb_kv, nsa_q, cmp_kv, sel_kv, win_kv, gates = _proj_in(x, w['g_attn_pre'][l], sh_a, sc_a, w['w_in_b'], ts, tm)
    per_seq = lambda a: a.reshape(db, ts, a.shape[1])
    new_rows = lambda a: jnp.pad(per_seq(a), ((0, 0), (0, QB - ts), (0, 0)))
    pp = _tile(n_pages, 8)
    o_sb = _sb_sample(page_table, per_seq(sb_q), new_rows(sb_kv), cache_sb)
    total = past + ts + (-(past + ts)) % SEL_BLOCK
    ncp = -(-(total // CMP_BLOCK + 1) // LANES) * LANES
    wbuf = state_win.shape[1]
    bias_c, bias_s, bias_w = _sample_bias_tables(w['rel_bias'], past, ts, n_pages, ncp, wbuf)
    wrow = jnp.tile(w['w_cmp_pos'][l], (1, QB // CMP_BLOCK))
    nq = per_seq(nsa_q)
    o_cmp, sel32 = _cmp_sample(page_table, nq, new_rows(cmp_kv), cache_cmp.reshape(-1, page, 2 * D_KV), wrow, bias_c, past, pp)
    o_nsa = _selwin_sample(page_table, nq, new_rows(sel_kv), sel32, bias_s, o_cmp, per_seq(gates),
                           state_win.reshape(db, wbuf, 2 * D_KV), new_rows(win_kv), bias_w,
                           cache_sel.reshape(-1, page, 2 * D_KV), pp)
    flat = lambda o: o[:, :ts].reshape(ns, o.shape[2])
    xa, hp, logits = _merge(x, flat(o_sb), flat(o_nsa), gt_a, sh_f, sc_f, w['g_sb_out'][l], w['g_nsa_out'][l],
                            w['g_attn_post'][l], w['g_ffn_pre'][l], w['w_out_b'], w['w_r'], w['b_r'], ts, tm)
    y = _moe(hp, logits, xa, gt_f, w['g_ffn_post'][l], w['w_gu'][l], w['b_gu'][l], w['w_dn'][l], w['b_dn'][l],
             ts, 128, _tile(ns, 512), tm)
    kv5 = lambda a, heads: a.reshape(db, ts, 2, heads, HEAD_DIM)
    win_new = jnp.concatenate([state_win, kv5(win_kv, G_NSA)], axis=1)[:, ts:]
    return y.reshape(db, ts, d), kv5(sb_kv, H_SB), kv5(cmp_kv, G_NSA), kv5(sel_kv, G_NSA), win_new


def kernel(x_prompt, x_sample, cache_sb_kv, cache_cmp_kv, cache_sel_kv, state_win_kv, page_table, c_prompt, c_sample, w_ada, b_ada, g_attn_pre, g_attn_post, w_in, w_cmp_pos, rel_bias, g_sb_out, g_nsa_out, w_out, g_ffn_pre, g_ffn_post, w_router, b_router, w_gu, b_gu, w_dn, b_dn):
    depth = w_in.shape[0]
    assert depth == 1
    b = x_prompt.shape[0]
    xp, xs = x_prompt, x_sample
    outs = [[] for _ in range(8)]
    for l in range(depth):
        w = _prep_weights(w_in, w_out, w_router, b_router, l)
        w.update(g_attn_pre=g_attn_pre, g_attn_post=g_attn_post, w_cmp_pos=w_cmp_pos, rel_bias=rel_bias, g_sb_out=g_sb_out,
                 g_nsa_out=g_nsa_out, g_ffn_pre=g_ffn_pre, g_ffn_post=g_ffn_post, w_gu=w_gu, b_gu=b_gu, w_dn=w_dn, b_dn=b_dn)
        mod = _adaln(jnp.concatenate([c_prompt, c_sample], axis=0), w_ada[l], b_ada[l])
        xp, sb_p, cmp_p, sel_p, win_p = _prompt_layer(xp, jnp.split(mod[:b], 6, axis=-1), w, l)
        xs, sb_s, cmp_s, sel_s, win_s = _sample_layer(
            xs, jnp.split(mod[b:], 6, axis=-1), (cache_sb_kv[l], cache_cmp_kv[l], cache_sel_kv[l]), state_win_kv[l],
            page_table, w, l)
        for o, v in zip(outs, (sb_p, sb_s, cmp_p, cmp_s, sel_p, sel_s, win_p, win_s)):
            o.append(v)
    return (xp, xs) + tuple(jnp.stack(o) for o in outs)
```

```python
import functools
import math

import jax
import jax.numpy as jnp
from jax import lax
from jax.experimental import pallas as pl
from jax.experimental.pallas import tpu as pltpu

F32 = jnp.float32
BF16 = jnp.bfloat16
I32 = jnp.int32
HI = lax.Precision.HIGHEST

HEAD_DIM = 64
H_SB = 8
H_NSA = 8
G_NSA = 2
R_NSA = H_NSA // G_NSA
D_SB = H_SB * HEAD_DIM
D_NSA = H_NSA * HEAD_DIM
D_KV = G_NSA * HEAD_DIM
CMP_BLOCK = 32
SEL_BLOCK = 64
TOP_N = 16
WINDOW = 512
N_BUCKETS = 32
MAX_DISTANCE = 128
N_EXPERTS = 32
TOP_K = 4
SWIGLU_LIMIT = 7.0
SWIGLU_ALPHA = 1.702
EPS = 1e-6
MASK_VALUE = -1e30
SCALE = HEAD_DIM ** -0.5

LANES = 128
QB = 128
GATE_PAD = LANES
VMEM_LIMIT = 56 * 1024 * 1024


def _cparams(sem, vmem=VMEM_LIMIT):
    return pltpu.CompilerParams(dimension_semantics=sem, vmem_limit_bytes=vmem)


def _dot(a, b, precision=None):
    return jnp.dot(a, b, preferred_element_type=F32, precision=precision)


def _dot_nt(a, b, precision=None):
    return lax.dot_general(a, b, (((1,), (1,)), ((), ())), preferred_element_type=F32, precision=precision)


def _iota(shape, axis):
    return lax.broadcasted_iota(I32, shape, axis)


def _rms(x, g):
    return x * lax.rsqrt(jnp.mean(x * x, axis=-1, keepdims=True) + EPS) * g


def _t5_bucket(dist):
    n = jnp.maximum(dist, 0)
    max_exact = N_BUCKETS // 2
    log_ratio = jnp.log(jnp.maximum(n, 1).astype(F32) / max_exact) / math.log(MAX_DISTANCE / max_exact)
    large = jnp.minimum(max_exact + (log_ratio * (N_BUCKETS - max_exact)).astype(I32), N_BUCKETS - 1)
    return jnp.where(n < max_exact, n, large)


def _adaln_kernel(c_ref, w_ref, b_ref, o_ref):
    c = c_ref[...]
    o_ref[...] = _dot(c * jax.nn.sigmoid(c), w_ref[...], HI) + b_ref[...]


def _adaln(c, w, b):
    r, d = c.shape
    n = w.shape[1]
    tn = 1024
    return pl.pallas_call(
        _adaln_kernel, grid=(n // tn,),
        in_specs=[pl.BlockSpec((r, d), lambda j: (0, 0)), pl.BlockSpec((d, tn), lambda j: (0, j)),
                  pl.BlockSpec((1, tn), lambda j: (0, j))],
        out_specs=pl.BlockSpec((r, tn), lambda j: (0, j)),
        out_shape=jax.ShapeDtypeStruct((r, n), F32), compiler_params=_cparams(("arbitrary",)), name="adaln",
    )(c, w, b.reshape(1, n))


def _mod_spec(mod, tm, rows_per_group):
    g, r, d = mod.shape
    tiles = rows_per_group // tm if r == 1 else 1
    return pl.BlockSpec((1, r, d), lambda i: (i // tiles, 0, 0))


_SPLITS = (D_SB, 2 * D_SB, D_NSA, 2 * D_KV, 2 * D_KV, 2 * D_KV, GATE_PAD)


def _proj_kernel(x_ref, g_ref, sh_ref, sc_ref, w_ref, *out_refs):
    h = (_rms(x_ref[...], g_ref[...]) * (1.0 + sc_ref[0]) + sh_ref[0]).astype(BF16)
    off = 0
    for k, (ref, n) in enumerate(zip(out_refs, _SPLITS)):
        r = _dot(h, w_ref[:, off:off + n])
        ref[...] = jax.nn.sigmoid(r) if k == len(_SPLITS) - 1 else r
        off += n


def _proj_in(x, g_pre, shift, scale, w_in_b, rows_per_group, tm):
    n, d = x.shape
    nw = w_in_b.shape[1]
    row = lambda c: pl.BlockSpec((tm, c), lambda i: (i, 0))
    return pl.pallas_call(
        _proj_kernel, grid=(n // tm,),
        in_specs=[row(d), pl.BlockSpec((1, d), lambda i: (0, 0)), _mod_spec(shift, tm, rows_per_group),
                  _mod_spec(scale, tm, rows_per_group), pl.BlockSpec((d, nw), lambda i: (0, 0))],
        out_specs=[row(c) for c in _SPLITS],
        out_shape=[jax.ShapeDtypeStruct((n, c), F32) for c in _SPLITS],
        compiler_params=_cparams(("arbitrary",)), name="proj_in",
    )(x, g_pre.reshape(1, d), shift, scale, w_in_b)


def _sb_terms(z):
    l = jnp.log(1.0 + jnp.exp(-jnp.abs(z)))
    ls = jnp.minimum(z, 0.0) - l
    return ls, ls - z


def _cumsum_rhs(blk):
    r, c = _iota((blk, 2 * blk), 0), _iota((blk, 2 * blk), 1)
    return jnp.where((r > c) | (c >= blk), 1.0, 0.0).astype(BF16)


def _split_dot(x, rhs):
    hi = x.astype(BF16)
    lo = (x - hi.astype(F32)).astype(BF16)
    return _dot(hi, rhs) + _dot(lo, rhs)


SB_DEAD = -106.0


def _sb_prompt_kernel(q_ref, k_ref, v_ref, o_ref, acc_s, cum_s):
    i = pl.program_id(1)
    lane = _iota((QB, LANES), 1)
    row, col = _iota((QB, QB), 0), _iota((QB, QB), 1)
    u2 = _cumsum_rhs(QB)
    acc_s[...] = jnp.zeros(acc_s.shape, F32)
    cum_s[...] = jnp.zeros(cum_s.shape, F32)

    def step(j, diag):
        rows = pl.ds(pl.multiple_of(j * QB, QB), QB)
        worst = None
        for p in range(H_SB // 2):
            cols = slice(p * LANES, (p + 1) * LANES)
            kb = k_ref[rows, cols].astype(BF16)
            vb = v_ref[rows, cols].astype(BF16)
            q = q_ref[:, cols] * SCALE
            for hf in (0, 1):
                h = 2 * p + hf
                qh = jnp.where((lane >= HEAD_DIM) == (hf == 1), q, 0.0).astype(BF16)
                ls, lk = _sb_terms(_dot_nt(qh, kb))
                if diag:
                    lk = jnp.where(col < row, lk, 0.0)
                cs = _split_dot(lk, u2)
                a = jnp.exp(ls + cs[:, :QB] + cum_s[h])
                if diag:
                    a = jnp.where(col < row, a, 0.0)
                acc_s[h] = acc_s[h] + _dot(a.astype(BF16), vb)
                cum = cum_s[h] + cs[:, QB:]
                cum_s[h] = cum
                worst = cum if worst is None else jnp.maximum(worst, cum)
        return jnp.max(worst)

    def cond(c):
        return jnp.logical_and(c[0] >= 0, c[1] > SB_DEAD)

    lax.while_loop(cond, lambda c: (c[0] - 1, step(c[0], False)), (i - 1, step(i, True)))
    for p in range(H_SB // 2):
        o_ref[:, p * LANES:(p + 1) * LANES] = jnp.where(lane < HEAD_DIM, acc_s[2 * p], acc_s[2 * p + 1])


def _sb_prompt(sb_q, sb_kv, b, t):
    nq = t // QB
    return pl.pallas_call(
        _sb_prompt_kernel, grid=(b, nq),
        in_specs=[pl.BlockSpec((QB, D_SB), lambda bi, i: (bi * nq + i, 0)),
                  pl.BlockSpec((t, D_SB), lambda bi, i: (bi, 0)),
                  pl.BlockSpec((t, D_SB), lambda bi, i: (bi, 1))],
        out_specs=pl.BlockSpec((QB, D_SB), lambda bi, i: (bi * nq + i, 0)),
        out_shape=jax.ShapeDtypeStruct((b * t, D_SB), F32),
        scratch_shapes=[pltpu.VMEM((H_SB, QB, LANES), F32)] * 2,
        compiler_params=_cparams(("arbitrary", "arbitrary")), name="sb_prompt",
    )(sb_q, sb_kv, sb_kv)


def _head_to_group_half(x2, h, lane):
    g = h // R_NSA
    if (h % 2) != g:
        x2 = pltpu.roll(x2, HEAD_DIM, axis=1)
    return jnp.where((lane >= HEAD_DIM) == (g == 1), x2, 0.0)


def _pair_tile(res, p, lane):
    g = (2 * p) // R_NSA
    a = res[2 * p] if g == 0 else pltpu.roll(res[2 * p], HEAD_DIM, axis=1)
    b = res[2 * p + 1] if g == 1 else pltpu.roll(res[2 * p + 1], HEAD_DIM, axis=1)
    return jnp.where(lane < HEAD_DIM, a, b)


def _cmp_prompt_kernel(q_ref, ckv_ref, wt_ref, fc_ref, ocmp_ref, imp_ref, ck_s, cv_s, *, t):
    i = pl.program_id(1)
    n_cmp, n_sel = t // CMP_BLOCK, t // SEL_BLOCK

    @pl.when(i == 0)
    def _():
        onb = (_iota((n_cmp, t), 1) // CMP_BLOCK) == _iota((n_cmp, t), 0)
        ck_s[...] = _dot(jnp.where(onb, wt_ref[0:1, :], 0.0), ckv_ref[:, 0:D_KV], HI)
        cv_s[...] = _dot(jnp.where(onb, wt_ref[1:2, :], 0.0), ckv_ref[:, D_KV:2 * D_KV], HI)

    lane = _iota((QB, LANES), 1)
    tpos = i * QB + _iota((QB, n_cmp), 0)
    valid = tpos >= _iota((QB, n_cmp), 1) * CMP_BLOCK + (CMP_BLOCK - 1)
    ck = ck_s[...]
    cvb = cv_s[...].astype(BF16)
    shift = (4 * i - 4 + n_cmp) % n_cmp
    psum = [jnp.zeros((QB, n_cmp), F32) for _ in range(G_NSA)]
    res = []
    for h in range(H_NSA):
        g, p = h // R_NSA, h // 2
        qh = _head_to_group_half(q_ref[:, p * LANES:(p + 1) * LANES], h, lane)
        s = _dot_nt(qh, ck, HI) * SCALE + pltpu.roll(fc_ref[h], shift, axis=1)
        s = jnp.where(valid, s, MASK_VALUE)
        pr = jnp.where(valid, jnp.exp(s - jnp.max(s, axis=-1, keepdims=True)), 0.0)
        pr = pr / jnp.maximum(jnp.sum(pr, axis=-1, keepdims=True), 1e-30)
        psum[g] = psum[g] + pr
        res.append(_dot(pr.astype(BF16), cvb))
    for p in range(H_NSA // 2):
        ocmp_ref[:, p * LANES:(p + 1) * LANES] = _pair_tile(res, p, lane)
    pr_, pc_ = _iota((n_cmp, 2 * n_sel), 0), _iota((n_cmp, 2 * n_sel), 1)
    per = SEL_BLOCK // CMP_BLOCK
    imp = (_dot(psum[0], jnp.where(pc_ == pr_ // per, 1.0, 0.0), HI)
           + _dot(psum[1], jnp.where(pc_ == n_sel + pr_ // per, 1.0, 0.0), HI))
    tq = i * QB + _iota((QB, 2 * n_sel), 0)
    j = _iota((QB, 2 * n_sel), 1) % n_sel
    imp_ref[...] = jnp.where(tq // SEL_BLOCK == j, jnp.inf, jnp.where(j * SEL_BLOCK > tq, -jnp.inf, imp))


def _cmp_prompt(nsa_q, cmp_kv, wt, fc, b, t):
    nq = t // QB
    n_cmp, n_sel = t // CMP_BLOCK, t // SEL_BLOCK
    return pl.pallas_call(
        functools.partial(_cmp_prompt_kernel, t=t), grid=(b, nq),
        in_specs=[pl.BlockSpec((QB, D_NSA), lambda bi, i: (bi * nq + i, 0)),
                  pl.BlockSpec((t, 2 * D_KV), lambda bi, i: (bi, 0)),
                  pl.BlockSpec((2, t), lambda bi, i: (0, 0)),
                  pl.BlockSpec((H_NSA, QB, n_cmp), lambda bi, i: (0, 0, 0))],
        out_specs=[pl.BlockSpec((QB, D_NSA), lambda bi, i: (bi * nq + i, 0)),
                   pl.BlockSpec((QB, 2 * n_sel), lambda bi, i: (bi * nq + i, 0))],
        out_shape=[jax.ShapeDtypeStruct((b * t, D_NSA), F32), jax.ShapeDtypeStruct((b * t, 2 * n_sel), F32)],
        scratch_shapes=[pltpu.VMEM((n_cmp, D_KV), F32), pltpu.VMEM((n_cmp, D_KV), F32)],
        compiler_params=_cparams(("arbitrary", "arbitrary")), name="nsa_cmp_prompt",
    )(nsa_q, cmp_kv, wt, fc)


def _topn_mask_t(imp_t, n_sel):
    jrow = _iota(imp_t.shape, 0) % n_sel
    rank = jnp.zeros(imp_t.shape, F32)
    w = imp_t.shape[1]
    for ii in range(n_sel):
        a = jnp.concatenate([jnp.broadcast_to(imp_t[ii:ii + 1, :], (n_sel, w)),
                             jnp.broadcast_to(imp_t[n_sel + ii:n_sel + ii + 1, :], (n_sel, w))], axis=0)
        rank = rank + jnp.where(jrow > ii, jnp.where(a >= imp_t, 1.0, 0.0), jnp.where(a > imp_t, 1.0, 0.0))
    return jnp.where(rank < min(TOP_N, n_sel), 1.0, 0.0)


def _flash_init(m_s, l_s, acc_s):
    m_s[...] = jnp.full(m_s.shape, MASK_VALUE, F32)
    l_s[...] = jnp.zeros(l_s.shape, F32)
    acc_s[...] = jnp.zeros(acc_s.shape, F32)


def _flash_update(s, vb, m_s, l_s, acc_s):
    m_old = m_s[...]
    m_new = jnp.maximum(m_old, jnp.max(s, axis=-1, keepdims=True))
    alpha = jnp.exp(m_old - m_new)
    p = jnp.exp(s - m_new)
    l_s[...] = alpha * l_s[...] + jnp.sum(p, axis=-1, keepdims=True)
    acc_s[...] = alpha * acc_s[...] + _dot(p.astype(BF16), vb)
    m_s[...] = m_new


def _flash_result(l_s, acc_s):
    return acc_s[...] / jnp.maximum(l_s[...], 1e-30)


def _gate_expand(gates, br):
    r, c = _iota((GATE_PAD, D_NSA), 0), _iota((GATE_PAD, D_NSA), 1)
    return _dot(gates, jnp.where(r == 3 * (c // HEAD_DIM) + br, 1.0, 0.0), HI)


SEL_CHUNK = 4 * QB


def _biased_masked(s, madd, tz_ref, g, i, first_block):
    out = []
    for u in range(s.shape[1] // QB):
        cs = slice(u * QB, (u + 1) * QB)
        kind = jnp.clip(i - (first_block + u), 0, 2)
        out.append(s[:, cs] + tz_ref[g, kind] + jnp.concatenate([madd[:, cs]] * R_NSA, axis=0))
    return jnp.concatenate(out, axis=1)


def _selwin_kernel(q_ref, imp_ref, gates_ref, ocmp_ref, skv_ref, wkv_ref, tz_ref, e_ref, o_ref, m_s, l_s, acc_s, mask_s, *, t):
    i = pl.program_id(1)
    n_sel = t // SEL_BLOCK
    lane = _iota((QB, LANES), 1)
    sel = _topn_mask_t(imp_ref[...].T, n_sel).T.astype(BF16)
    n_back = WINDOW // QB
    o_sel, o_win = [], []
    for g in range(G_NSA):
        qg = jnp.concatenate(
            [_head_to_group_half(q_ref[:, (h // 2) * LANES:(h // 2 + 1) * LANES] * SCALE, h, lane).astype(BF16)
             for h in range(g * R_NSA, (g + 1) * R_NSA)], axis=0)

        for c in range(t // SEL_CHUNK):
            cs = slice(c * SEL_CHUNK, (c + 1) * SEL_CHUNK)
            mask_s[:, cs] = jnp.where(_dot(sel, e_ref[g, :, cs]) > 0.5, 0.0, MASK_VALUE)
        _flash_init(m_s, l_s, acc_s)
        qpos = i * QB + _iota((QB, SEL_CHUNK), 0)

        def sel_body(c, _):
            start = pl.multiple_of(c * SEL_CHUNK, SEL_CHUNK)
            kb = skv_ref[pl.ds(start, SEL_CHUNK), 0:D_KV].astype(BF16)
            vb = skv_ref[pl.ds(start, SEL_CHUNK), D_KV:2 * D_KV].astype(BF16)
            causal = jnp.where(start + _iota((QB, SEL_CHUNK), 1) <= qpos, 0.0, MASK_VALUE)
            s = _biased_masked(_dot_nt(qg, kb), mask_s[:, pl.ds(start, SEL_CHUNK)] + causal, tz_ref, g, i,
                               c * (SEL_CHUNK // QB))
            m_old = m_s[...]
            m_new = jnp.maximum(m_old, jnp.max(s, axis=-1, keepdims=True))
            alpha = jnp.exp(m_old - m_new)
            p = jnp.exp(s - jnp.concatenate([m_new] * (SEL_CHUNK // LANES), axis=1))
            l_s[...] = alpha * l_s[...] + jnp.sum(p, axis=-1, keepdims=True)
            acc_s[...] = alpha * acc_s[...] + _dot(p.astype(BF16), vb)
            m_s[...] = m_new
            return 0

        lax.fori_loop(0, (i * QB + QB + SEL_CHUNK - 1) // SEL_CHUNK, sel_body, 0)
        o_sel.append(_flash_result(l_s, acc_s))

        first = jnp.maximum(i - n_back, 0)
        wk = (n_back + 1) * QB
        start = pl.multiple_of(first * QB, QB)
        kb = wkv_ref[pl.ds(start, wk), 0:D_KV].astype(BF16)
        vb = wkv_ref[pl.ds(start, wk), D_KV:2 * D_KV].astype(BF16)
        dist = i * QB + _iota((QB, wk), 0) - (start + _iota((QB, wk), 1))
        inside = jnp.where(dist >= 0, jnp.where(dist < WINDOW, 0.0, MASK_VALUE), MASK_VALUE)
        s = _biased_masked(_dot_nt(qg, kb), inside, tz_ref, g, i, first)
        p = jnp.exp(s - jnp.max(s, axis=-1, keepdims=True))
        o_win.append(_dot(p.astype(BF16), vb) / jnp.maximum(jnp.sum(p, axis=-1, keepdims=True), 1e-30))

    def heads(o):
        return [o[h // R_NSA][(h % R_NSA) * QB:(h % R_NSA + 1) * QB, :] for h in range(H_NSA)]

    hs, hw = heads(o_sel), heads(o_win)
    gates = gates_ref[...]
    g_cmp, g_sel, g_win = [_gate_expand(gates, br) for br in range(3)]
    for p in range(H_NSA // 2):
        cs = slice(p * LANES, (p + 1) * LANES)
        o_ref[:, cs] = (g_cmp[:, cs] * ocmp_ref[:, cs] + g_sel[:, cs] * _pair_tile(hs, p, lane)
                        + g_win[:, cs] * _pair_tile(hw, p, lane))


def _selwin_prompt(nsa_q, imp, gates, o_cmp, sel_kv, win_kv, tz, b, t):
    nq = t // QB
    n_sel = t // SEL_BLOCK
    assert t % SEL_CHUNK == 0 and t >= WINDOW + QB
    rowspec = lambda c: pl.BlockSpec((QB, c), lambda bi, i: (bi * nq + i, 0))
    seqspec = pl.BlockSpec((t, 2 * D_KV), lambda bi, i: (bi, 0))
    key_block = jnp.arange(t, dtype=I32) // SEL_BLOCK
    expand = (jnp.arange(2 * n_sel, dtype=I32)[None, :, None]
              == jnp.arange(G_NSA, dtype=I32)[:, None, None] * n_sel + key_block[None, None, :]).astype(BF16)
    return pl.pallas_call(
        functools.partial(_selwin_kernel, t=t), grid=(b, nq),
        in_specs=[rowspec(D_NSA), rowspec(2 * n_sel), rowspec(GATE_PAD), rowspec(D_NSA), seqspec, seqspec,
                  pl.BlockSpec((G_NSA, 3, R_NSA * QB, QB), lambda bi, i: (0, 0, 0, 0)),
                  pl.BlockSpec((G_NSA, 2 * n_sel, t), lambda bi, i: (0, 0, 0))],
        out_specs=rowspec(D_NSA),
        out_shape=jax.ShapeDtypeStruct((b * t, D_NSA), F32),
        scratch_shapes=[pltpu.VMEM((R_NSA * QB, LANES), F32)] * 3 + [pltpu.VMEM((QB, t), F32)],
        compiler_params=_cparams(("arbitrary", "arbitrary")), name="nsa_selwin_prompt",
    )(nsa_q, imp, gates, o_cmp, sel_kv, win_kv, tz, expand)


def _head_rows(q, ts):
    rows = ts * H_SB
    qs = jnp.concatenate([jnp.broadcast_to(q[t:t + 1], (H_SB, q.shape[1])) for t in range(ts)], axis=0)
    return jnp.where(_iota((rows, q.shape[1]), 1) // HEAD_DIM == _iota((rows, q.shape[1]), 0) % H_SB, qs, 0.0)


def _collapse_rows(x, ts):
    rows = ts * H_SB
    x = jnp.where(_iota((rows, x.shape[1]), 1) // HEAD_DIM == _iota((rows, x.shape[1]), 0) % H_SB, x, 0.0)
    out = [jnp.sum(x[t * H_SB:(t + 1) * H_SB], axis=0, keepdims=True) for t in range(ts)]
    return jnp.concatenate(out + [jnp.zeros((8 - ts, x.shape[1]), F32)], axis=0)


def _fold_q(q, ts):
    fr, fc = _iota((D_NSA, LANES), 0), _iota((D_NSA, LANES), 1)
    fold = jnp.where(fc == ((fr // HEAD_DIM) // R_NSA) * HEAD_DIM + fr % HEAD_DIM, 1.0, 0.0)
    return _dot(_head_rows(q, ts), fold, HI)


def _unfold_rows(o, ts):
    ur, uc = _iota((LANES, D_NSA), 0), _iota((LANES, D_NSA), 1)
    unf = jnp.where(ur % HEAD_DIM == uc % HEAD_DIM, jnp.where(ur // HEAD_DIM == (uc // HEAD_DIM) // R_NSA, 1.0, 0.0), 0.0)
    return _collapse_rows(_dot(o, unf, HI), ts)


def _keys_minor(cache):
    return jnp.transpose(cache, (0, 1, 3, 4, 5, 2))


def _sb_sample_kernel(pt_ref, q_ref, new_ref, cache_ref, o_ref, buf, sem, cum_s, acc_s, *, ts, n_pages, layer):
    b = pl.program_id(0)
    rows = ts * H_SB
    u2 = _cumsum_rhs(QB)

    def page_copy(k, slot):
        return pltpu.make_async_copy(cache_ref.at[layer, pt_ref[b, n_pages - 1 - k]], buf.at[slot], sem.at[slot])

    page_copy(0, 0).start()

    qs = _head_rows(q_ref[0] * SCALE, ts).astype(BF16)
    new = new_ref[0]
    valid = _iota((rows, QB), 1) < _iota((rows, QB), 0) // H_SB
    ls, lk = _sb_terms(_dot_nt(qs, new[:, 0:D_SB].astype(BF16)))
    cs = _split_dot(jnp.where(valid, lk, 0.0), u2)
    a = jnp.where(valid, jnp.exp(ls + cs[:, :QB]), 0.0)
    acc_s[...] = _dot(a.astype(BF16), new[:, D_SB:2 * D_SB].astype(BF16))
    cum_s[...] = cs[:, QB:]

    def walk(c):
        k, _ = c
        slot = k % 2
        page_copy(k, slot).wait()

        @pl.when(k + 1 < n_pages)
        def _():
            page_copy(k + 1, 1 - slot).start()

        kt = buf[slot, 0].reshape(D_SB, QB).astype(BF16)
        vt = buf[slot, 1].reshape(D_SB, QB).astype(BF16)
        ls, lk = _sb_terms(_dot(qs, kt))
        cs = _split_dot(lk, u2)
        a = jnp.exp(ls + cs[:, :QB] + cum_s[...])
        acc_s[...] = acc_s[...] + _dot_nt(a.astype(BF16), vt)
        cum = cum_s[...] + cs[:, QB:]
        cum_s[...] = cum
        return k + 1, jnp.max(cum)

    k_end, _ = lax.while_loop(lambda c: jnp.logical_and(c[0] < n_pages, c[1] > SB_DEAD), walk,
                              (jnp.int32(0), jnp.max(cum_s[...])))

    @pl.when(k_end < n_pages)
    def _():
        page_copy(k_end, k_end % 2).wait()

    o_ref[0] = _collapse_rows(acc_s[...], ts)


def _paged_specs(cache, layer, pp):
    def spec(ip):
        return pl.BlockSpec((1, 1) + tuple(cache.shape[2:]), lambda b, c, pt: (layer, pt[b, c * pp + ip], 0, 0, 0, 0))
    return [spec(ip) for ip in range(pp)]


def _page_kt_vt(page_ref):
    return page_ref[0, 0, 0].reshape(D_KV, QB), page_ref[0, 0, 1].reshape(D_KV, QB)


def _batch_spec(shape):
    return pl.BlockSpec((1,) + tuple(shape[1:]), lambda b, c, pt: (b,) + (0,) * (len(shape) - 1))


def _const_spec(shape):
    return pl.BlockSpec(tuple(shape), lambda b, c, pt: (0,) * len(shape))


def _sb_sample(page_table, q, new_kv, cache, layer):
    db, ts, _ = q.shape
    n_pages = page_table.shape[1]
    rows = ts * H_SB
    per_seq = lambda shape: pl.BlockSpec((1,) + tuple(shape[1:]), lambda b, pt: (b,) + (0,) * (len(shape) - 1))
    grid_spec = pltpu.PrefetchScalarGridSpec(
        num_scalar_prefetch=1, grid=(db,),
        in_specs=[per_seq(q.shape), per_seq(new_kv.shape), pl.BlockSpec(memory_space=pl.ANY)],
        out_specs=per_seq((db, 8, D_SB)),
        scratch_shapes=[pltpu.VMEM((2,) + tuple(cache.shape[2:]), F32), pltpu.SemaphoreType.DMA((2,)),
                        pltpu.VMEM((rows, QB), F32), pltpu.VMEM((rows, D_SB), F32)])
    return pl.pallas_call(
        functools.partial(_sb_sample_kernel, ts=ts, n_pages=n_pages, layer=layer), grid_spec=grid_spec,
        out_shape=jax.ShapeDtypeStruct((db, 8, D_SB), F32),
        compiler_params=_cparams(("arbitrary",)), name="sb_sample",
    )(page_table, q, new_kv, cache)


def _cmp_sample_kernel(pt_ref, q_ref, new_ref, wrow_ref, bias_ref, *rest, pp, ts, past, n_cmp, n_sel):
    del pt_ref
    caches, (ocmp_ref, sel_ref), (ck_s, cv_s) = rest[:pp], rest[pp:pp + 2], rest[pp + 2:]
    c, nc = pl.program_id(1), pl.num_programs(1)
    ncp, nsp = ck_s.shape[0], sel_ref.shape[2]
    rows = ts * H_NSA
    per_page = QB // CMP_BLOCK

    @pl.when(c == 0)
    def _():
        ck_s[...] = jnp.zeros(ck_s.shape, F32)
        cv_s[...] = jnp.zeros(cv_s.shape, F32)

    nrow, rcol = _iota((ncp, QB), 0), _iota((ncp, QB), 1)

    def add_rows(kv, first_block):
        on = nrow == jnp.minimum(first_block + rcol // CMP_BLOCK, n_cmp)
        ck_s[...] = ck_s[...] + _dot(jnp.where(on, wrow_ref[0:1, :], 0.0), kv[:, 0:D_KV], HI)
        cv_s[...] = cv_s[...] + _dot(jnp.where(on, wrow_ref[1:2, :], 0.0), kv[:, D_KV:2 * D_KV], HI)

    orow, ocol = _iota((pp * per_page, QB), 0), _iota((pp * per_page, QB), 1)
    ck_new = jnp.zeros((pp * per_page, D_KV), F32)
    cv_new = jnp.zeros((pp * per_page, D_KV), F32)
    for ip in range(pp):
        on = orow == ip * per_page + ocol // CMP_BLOCK
        kt, vt = _page_kt_vt(caches[ip])
        ck_new = ck_new + _dot_nt(jnp.where(on, wrow_ref[0:1, :], 0.0), kt, HI)
        cv_new = cv_new + _dot_nt(jnp.where(on, wrow_ref[1:2, :], 0.0), vt, HI)
    out_rows = pl.ds(pl.multiple_of(c * (pp * per_page), pp * per_page), pp * per_page)
    ck_s[out_rows, :] = ck_new
    cv_s[out_rows, :] = cv_new

    @pl.when(c == nc - 1)
    def _():
        add_rows(new_ref[0], past // CMP_BLOCK)
        qc = _fold_q(q_ref[0], ts)
        ncol, tq = _iota((rows, ncp), 1), _iota((rows, ncp), 0) // H_NSA
        valid = ncol * CMP_BLOCK + (CMP_BLOCK - 1) <= past + tq
        s = jnp.where(valid, _dot_nt(qc, ck_s[...], HI) * SCALE + bias_ref[...], MASK_VALUE)
        pr = jnp.where(valid, jnp.exp(s - jnp.max(s, axis=-1, keepdims=True)), 0.0)
        pr = pr / jnp.maximum(jnp.sum(pr, axis=-1, keepdims=True), 1e-30)
        ocmp_ref[0] = _unfold_rows(_dot(pr.astype(BF16), cv_s[...].astype(BF16)), ts)
        gr, gc = _iota((QB, rows), 0), _iota((QB, rows), 1)
        gsum = jnp.where(gr < G_NSA * ts, jnp.where(gc == (gr % ts) * H_NSA + (gr // ts) * R_NSA + gc % R_NSA, 1.0, 0.0), 0.0)
        pr_, pc_ = _iota((ncp, nsp), 0), _iota((ncp, nsp), 1)
        imp = _dot(_dot(gsum, pr, HI), jnp.where(pc_ == pr_ // (SEL_BLOCK // CMP_BLOCK), 1.0, 0.0), HI)
        jj, tt = _iota((QB, nsp), 1), past + _iota((QB, nsp), 0) % ts
        imp = jnp.where(tt // SEL_BLOCK == jj, jnp.inf, jnp.where(jj * SEL_BLOCK > tt, -jnp.inf, imp))
        imp_t = imp.T
        ii, j2 = _iota((nsp, nsp), 0), _iota((nsp, nsp), 1)
        picks = []
        for r in range(G_NSA * ts):
            a = jnp.broadcast_to(imp_t[:, r:r + 1], (nsp, nsp))
            bb = jnp.broadcast_to(imp[r:r + 1, :], (nsp, nsp))
            beats = jnp.where(ii < j2, jnp.where(a >= bb, 1.0, 0.0), jnp.where(a > bb, 1.0, 0.0))
            picks.append(jnp.where(jnp.sum(beats, axis=0, keepdims=True) < min(TOP_N, n_sel), 1.0, 0.0))
        sel8 = jnp.concatenate(picks + [jnp.zeros((QB - G_NSA * ts, nsp), F32)], axis=0).astype(BF16)
        er, ec = _iota((rows, QB), 0), _iota((rows, QB), 1)
        expand = jnp.where(ec == ((er % H_NSA) // R_NSA) * ts + er // H_NSA, 1.0, 0.0).astype(BF16)
        sel_ref[0] = _dot(expand, sel8)


def _cmp_sample(page_table, q, new_kv, cache, layer, wrow, bias_c, past, pp):
    db, ts, _ = q.shape
    n_pages = page_table.shape[1]
    nc = n_pages // pp
    total = past + ts + (-(past + ts)) % SEL_BLOCK
    n_cmp, n_sel = total // CMP_BLOCK, total // SEL_BLOCK
    ncp = bias_c.shape[1]
    nsp = -(-n_sel // LANES) * LANES
    grid_spec = pltpu.PrefetchScalarGridSpec(
        num_scalar_prefetch=1, grid=(db, nc),
        in_specs=[_batch_spec(q.shape), _batch_spec(new_kv.shape), _const_spec(wrow.shape), _const_spec(bias_c.shape)]
        + _paged_specs(cache, layer, pp),
        out_specs=[_batch_spec((db, 8, D_NSA)), _batch_spec((db, ts * H_NSA, nsp))],
        scratch_shapes=[pltpu.VMEM((ncp, D_KV), F32), pltpu.VMEM((ncp, D_KV), F32)])
    return pl.pallas_call(
        functools.partial(_cmp_sample_kernel, pp=pp, ts=ts, past=past, n_cmp=n_cmp, n_sel=n_sel), grid_spec=grid_spec,
        out_shape=[jax.ShapeDtypeStruct((db, 8, D_NSA), F32), jax.ShapeDtypeStruct((db, ts * H_NSA, nsp), F32)],
        compiler_params=_cparams(("arbitrary", "arbitrary")), name="nsa_cmp_sample",
    )(page_table, q, new_kv, wrow, bias_c, *([cache] * pp))


def _selwin_sample_kernel(pt_ref, q_ref, newsel_ref, sel_ref, bias_ref, ocmp_ref, gates_ref, win_ref, newwin_ref,
                          biasw_ref, *rest, pp, ts, n_pages):
    del pt_ref
    caches, o_ref, (qc_s, m_s, l_s, acc_s) = rest[:pp], rest[pp], rest[pp + 1:]
    c, nc = pl.program_id(1), pl.num_programs(1)
    rows = ts * H_NSA
    nsp = sel_ref.shape[2]
    per_page = QB // SEL_BLOCK

    @pl.when(c == 0)
    def _():
        qc_s[...] = _fold_q(q_ref[0] * SCALE, ts).astype(BF16)
        _flash_init(m_s, l_s, acc_s)

    sel = sel_ref[0].astype(BF16)
    er, ec = _iota((nsp, QB), 0), _iota((nsp, QB), 1) // SEL_BLOCK
    kr, tr = _iota((rows, QB), 1), _iota((rows, QB), 0) // H_NSA

    def block(kv, page, allowed):
        picked = _dot(sel, jnp.where(er == per_page * page + ec, 1.0, 0.0).astype(BF16)) > 0.5
        s = _dot_nt(qc_s[...], kv[:, 0:D_KV].astype(BF16)) + bias_ref[page] + jnp.where(picked, allowed, MASK_VALUE)
        _flash_update(s, kv[:, D_KV:2 * D_KV].astype(BF16), m_s, l_s, acc_s)

    wide = pp * QB
    wr, wc = _iota((nsp, wide), 0), _iota((nsp, wide), 1) // SEL_BLOCK
    picked = _dot(sel, jnp.where(wr == per_page * pp * c + wc, 1.0, 0.0).astype(BF16)) > 0.5
    pages = [_page_kt_vt(caches[ip]) for ip in range(pp)]
    qc = qc_s[...]
    s = jnp.concatenate([_dot(qc, pages[ip][0].astype(BF16)) + bias_ref[c * pp + ip] for ip in range(pp)], axis=1)
    s = s + jnp.where(picked, 0.0, MASK_VALUE)
    m_old = m_s[...]
    m_new = jnp.maximum(m_old, jnp.max(s, axis=-1, keepdims=True))
    alpha = jnp.exp(m_old - m_new)
    p = jnp.exp(s - jnp.concatenate([m_new] * pp, axis=1))
    pv = sum(_dot_nt(p[:, ip * QB:(ip + 1) * QB].astype(BF16), pages[ip][1].astype(BF16)) for ip in range(pp))
    l_s[...] = alpha * l_s[...] + jnp.sum(p, axis=-1, keepdims=True)
    acc_s[...] = alpha * acc_s[...] + pv
    m_s[...] = m_new

    @pl.when(c == nc - 1)
    def _():
        new_allowed = jnp.where(kr <= tr, 0.0, MASK_VALUE)
        block(newsel_ref[0], n_pages, new_allowed)
        o_sel = _flash_result(l_s, acc_s)
        _flash_init(m_s, l_s, acc_s)
        w = win_ref.shape[1]
        for wb in range(w // QB):
            kv = win_ref[0, wb * QB:(wb + 1) * QB, :]
            s = _dot_nt(qc_s[...], kv[:, 0:D_KV].astype(BF16)) + biasw_ref[:, wb * QB:(wb + 1) * QB]
            if wb * QB <= w - WINDOW + ts - 1:
                s = s + jnp.where(wb * QB + kr > w - WINDOW + tr, 0.0, MASK_VALUE)
            _flash_update(s, kv[:, D_KV:2 * D_KV].astype(BF16), m_s, l_s, acc_s)
        kv = newwin_ref[0]
        s = _dot_nt(qc_s[...], kv[:, 0:D_KV].astype(BF16)) + biasw_ref[:, w:w + QB] + new_allowed
        _flash_update(s, kv[:, D_KV:2 * D_KV].astype(BF16), m_s, l_s, acc_s)
        o_win = _flash_result(l_s, acc_s)
        gates = jnp.concatenate([gates_ref[0], jnp.zeros((8 - ts, GATE_PAD), F32)], axis=0)
        o_ref[0] = (_gate_expand(gates, 0) * ocmp_ref[0] + _gate_expand(gates, 1) * _unfold_rows(o_sel, ts)
                    + _gate_expand(gates, 2) * _unfold_rows(o_win, ts))


def _selwin_sample(page_table, q, new_sel, sel32, bias_s, o_cmp, gates, state_win, new_win, bias_w, cache, layer, pp):
    db, ts, _ = q.shape
    n_pages = page_table.shape[1]
    nc = n_pages // pp
    rows = ts * H_NSA
    ins = [q, new_sel, sel32, bias_s, o_cmp, gates, state_win, new_win, bias_w]
    specs = [_batch_spec(q.shape), _batch_spec(new_sel.shape), _batch_spec(sel32.shape), _const_spec(bias_s.shape),
             _batch_spec(o_cmp.shape), _batch_spec(gates.shape), _batch_spec(state_win.shape), _batch_spec(new_win.shape),
             _const_spec(bias_w.shape)]
    grid_spec = pltpu.PrefetchScalarGridSpec(
        num_scalar_prefetch=1, grid=(db, nc),
        in_specs=specs + _paged_specs(cache, layer, pp),
        out_specs=_batch_spec((db, 8, D_NSA)),
        scratch_shapes=[pltpu.VMEM((rows, LANES), BF16)] + [pltpu.VMEM((rows, LANES), F32)] * 3)
    return pl.pallas_call(
        functools.partial(_selwin_sample_kernel, pp=pp, ts=ts, n_pages=n_pages), grid_spec=grid_spec,
        out_shape=jax.ShapeDtypeStruct((db, 8, D_NSA), F32),
        compiler_params=_cparams(("arbitrary", "arbitrary")), name="nsa_selwin_sample",
    )(page_table, *ins, *([cache] * pp))


def _merge_kernel(x_ref, osb_ref, onsa_ref, gt_ref, shf_ref, scf_ref, gsb_ref, gnsa_ref, gpost_ref, gpre_ref,
                  wout_ref, wr_ref, br_ref, xa_ref, hp_ref, lg_ref):
    a = _rms(osb_ref[...], gsb_ref[...]).astype(BF16)
    b = _rms(onsa_ref[...], gnsa_ref[...]).astype(BF16)
    y = _dot(a, wout_ref[0:D_SB, :]) + _dot(b, wout_ref[D_SB:D_SB + D_NSA, :])
    xa = x_ref[...] + gt_ref[0] * _rms(y, gpost_ref[...])
    xa_ref[...] = xa
    h = _rms(xa, gpre_ref[...]) * (1.0 + scf_ref[0]) + shf_ref[0]
    lg_ref[...] = _dot(h, wr_ref[...], HI) + br_ref[...]
    half = h.shape[1] // 2
    hb = pltpu.bitcast(h.astype(BF16).astype(F32), jnp.uint32)
    hp_ref[...] = hb[:, :half] | (hb[:, half:] >> 16)


def _merge(x, o_sb, o_nsa, gt_a, sh_f, sc_f, g_sb, g_nsa, g_post, g_pre, w_out_b, w_r, b_r, rows_per_group, tm):
    n, d = x.shape
    row = lambda c: pl.BlockSpec((tm, c), lambda i: (i, 0))
    full = lambda a: pl.BlockSpec(a.shape, lambda i: (0,) * a.ndim)
    vec = lambda v: v.reshape(1, -1)
    consts = [vec(g_sb), vec(g_nsa), vec(g_post), vec(g_pre), w_out_b, w_r, vec(b_r)]
    return pl.pallas_call(
        _merge_kernel, grid=(n // tm,),
        in_specs=[row(d), row(D_SB), row(D_NSA)] + [_mod_spec(m, tm, rows_per_group) for m in (gt_a, sh_f, sc_f)]
        + [full(a) for a in consts],
        out_specs=[row(d), row(d // 2), row(LANES)],
        out_shape=[jax.ShapeDtypeStruct((n, d), F32), jax.ShapeDtypeStruct((n, d // 2), jnp.uint32),
                   jax.ShapeDtypeStruct((n, LANES), F32)],
        compiler_params=_cparams(("arbitrary",)), name="merge_heads",
    )(x, o_sb, o_nsa, gt_a, sh_f, sc_f, *consts)


def _route_kernel(lg_ref, info_ref, cnt_ref, carry_s, *, tm):
    i = pl.program_id(0)

    @pl.when(i == 0)
    def _():
        carry_s[...] = jnp.zeros(carry_s.shape, F32)

    lane = _iota((tm, LANES), 1)
    l = jnp.where(lane < N_EXPERTS, lg_ref[...], -jnp.inf)
    tops, idxs, ohs = [], [], []
    for _ in range(TOP_K):
        m = jnp.max(l, axis=-1, keepdims=True)
        idx = jnp.min(jnp.where(l == m, lane, LANES), axis=-1, keepdims=True)
        oh = lane == idx
        l = jnp.where(oh, -jnp.inf, l)
        tops.append(m)
        idxs.append(idx)
        ohs.append(oh)
    e = [jnp.exp(v - tops[0]) for v in tops]
    den = e[0] + e[1] + e[2] + e[3]
    cnt = sum(jnp.where(oh, 1.0, 0.0) for oh in ohs)
    ltri = jnp.where(_iota((tm, tm), 0) > _iota((tm, tm), 1), 1.0, 0.0).astype(BF16)
    prefix = _dot(ltri, cnt.astype(BF16)) + carry_s[0:1, :]
    info = jnp.zeros((tm, LANES), F32)
    for k in range(TOP_K):
        rank = jnp.sum(jnp.where(ohs[k], prefix, 0.0), axis=-1, keepdims=True)
        info = jnp.where(lane == k, idxs[k].astype(F32), info)
        info = jnp.where(lane == TOP_K + k, rank, info)
        info = jnp.where(lane == 2 * TOP_K + k, e[k] / den, info)
    info_ref[...] = info
    carry_s[...] = carry_s[...] + jnp.sum(cnt, axis=0, keepdims=True)
    cnt_ref[...] = carry_s[...]


def _route(logits, tm):
    n = logits.shape[0]
    return pl.pallas_call(
        functools.partial(_route_kernel, tm=tm), grid=(n // tm,),
        in_specs=[pl.BlockSpec((tm, LANES), lambda i: (i, 0))],
        out_specs=[pl.BlockSpec((tm, LANES), lambda i: (i, 0)), pl.BlockSpec((8, LANES), lambda i: (0, 0))],
        out_shape=[jax.ShapeDtypeStruct((n, LANES), F32), jax.ShapeDtypeStruct((8, LANES), F32)],
        scratch_shapes=[pltpu.VMEM((8, LANES), F32)],
        compiler_params=_cparams(("arbitrary",)), name="moe_route",
    )(logits)


def _dispatch_kernel(dest_ref, hp_ref, xz_ref, xbuf_ref, sem, *, tm):
    del xz_ref

    def row_copy(t, d):
        return pltpu.make_async_copy(hp_ref.at[pl.ds(t, 1), :], xbuf_ref.at[pl.ds(d, 1), :], sem)

    def start(t, _):
        for k in range(TOP_K):
            row_copy(t, dest_ref[t * TOP_K + k]).start(priority=k % 2)
        return 0

    def wait(t, _):
        for k in range(TOP_K):
            row_copy(t, dest_ref[t * TOP_K + k]).wait()
        return 0

    lax.fori_loop(0, tm, start, 0)
    lax.fori_loop(0, tm, wait, 0)


def _dispatch(dest, hp, n_rows, tm):
    n, w = hp.shape
    xz = jnp.zeros((n_rows, w), hp.dtype)
    return pl.pallas_call(
        functools.partial(_dispatch_kernel, tm=tm), grid=(n // tm,),
        in_specs=[pl.BlockSpec((tm * TOP_K,), lambda i: (i,), memory_space=pltpu.SMEM),
                  pl.BlockSpec((tm, w), lambda i: (i, 0)), pl.BlockSpec(memory_space=pl.ANY)],
        out_specs=pl.BlockSpec(memory_space=pl.ANY),
        out_shape=jax.ShapeDtypeStruct((n_rows, w), hp.dtype),
        scratch_shapes=[pltpu.SemaphoreType.DMA(())],
        input_output_aliases={2: 0},
        compiler_params=_cparams(("arbitrary",)), name="moe_dispatch",
    )(dest, hp, xz)


def _ffn_kernel(be_ref, x_ref, wgu_ref, bgu_ref, wdn_ref, bdn_ref, y_ref, wgu_s, wdn_s):
    b = pl.program_id(0)
    changed = jnp.logical_or(b == 0, be_ref[b] != be_ref[jnp.maximum(b - 1, 0)])

    @pl.when(changed)
    def _():
        wgu_s[...] = wgu_ref[0].astype(BF16)
        wdn_s[...] = wdn_ref[0].astype(BF16)

    xw = x_ref[...]
    half = xw.shape[1]
    xa = pltpu.bitcast(xw & jnp.uint32(0xFFFF0000), F32).astype(BF16)
    xb = pltpu.bitcast(xw << 16, F32).astype(BF16)
    gu = _dot(xa, wgu_s[0:half, :]) + _dot(xb, wgu_s[half:2 * half, :]) + bgu_ref[0]
    dff = gu.shape[1] // 2
    glu = jnp.minimum(gu[:, :dff], SWIGLU_LIMIT)
    lin = jnp.clip(gu[:, dff:], -SWIGLU_LIMIT, SWIGLU_LIMIT)
    act = glu * jax.nn.sigmoid(SWIGLU_ALPHA * glu) * (lin + 1.0)
    y_ref[...] = _dot(act.astype(BF16), wdn_s[...]) + bdn_ref[0]


def _ffn(blk_expert, xbuf, w_gu, b_gu, w_dn, b_dn, blk):
    n_rows, half = xbuf.shape
    ne, d, dff2 = w_gu.shape
    dff = dff2 // 2
    grid_spec = pltpu.PrefetchScalarGridSpec(
        num_scalar_prefetch=1, grid=(n_rows // blk,),
        in_specs=[pl.BlockSpec((blk, half), lambda b, be: (b, 0)),
                  pl.BlockSpec((1, d, dff2), lambda b, be: (be[b], 0, 0)),
                  pl.BlockSpec((1, 1, dff2), lambda b, be: (be[b], 0, 0)),
                  pl.BlockSpec((1, dff, d), lambda b, be: (be[b], 0, 0)),
                  pl.BlockSpec((1, 1, d), lambda b, be: (be[b], 0, 0))],
        out_specs=pl.BlockSpec((blk, d), lambda b, be: (b, 0)),
        scratch_shapes=[pltpu.VMEM((d, dff2), BF16), pltpu.VMEM((dff, d), BF16)])
    return pl.pallas_call(
        _ffn_kernel, grid_spec=grid_spec, out_shape=jax.ShapeDtypeStruct((n_rows, d), F32),
        compiler_params=_cparams(("arbitrary",)), name="moe_ffn",
    )(blk_expert, xbuf, w_gu, b_gu.reshape(ne, 1, dff2), w_dn, b_dn.reshape(ne, 1, d))


def _combine_kernel(dest_ref, info_ref, xa_ref, gt_ref, gpost_ref, ybuf_ref, o_ref, rows_s, sem, *, tm):
    def row_copy(t, k):
        return pltpu.make_async_copy(ybuf_ref.at[pl.ds(dest_ref[t * TOP_K + k], 1), :], rows_s.at[k, pl.ds(t, 1), :], sem)

    def start(t, _):
        for k in range(TOP_K):
            row_copy(t, k).start(priority=k % 2)
        return 0

    def wait(t, _):
        for k in range(TOP_K):
            row_copy(t, k).wait()
        return 0

    lax.fori_loop(0, tm, start, 0)
    lax.fori_loop(0, tm, wait, 0)
    info = info_ref[...]
    y = sum(info[:, 2 * TOP_K + k:2 * TOP_K + k + 1] * rows_s[k] for k in range(TOP_K))
    o_ref[...] = xa_ref[...] + gt_ref[0] * _rms(y, gpost_ref[...])


def _combine(dest, info, xa, gt_f, g_post, ybuf, rows_per_group, tm):
    n, d = xa.shape
    row = lambda c: pl.BlockSpec((tm, c), lambda i: (i, 0))
    return pl.pallas_call(
        functools.partial(_combine_kernel, tm=tm), grid=(n // tm,),
        in_specs=[pl.BlockSpec((tm * TOP_K,), lambda i: (i,), memory_space=pltpu.SMEM), row(LANES), row(d),
                  _mod_spec(gt_f, tm, rows_per_group), pl.BlockSpec((1, d), lambda i: (0, 0)),
                  pl.BlockSpec(memory_space=pl.ANY)],
        out_specs=row(d),
        out_shape=jax.ShapeDtypeStruct((n, d), F32),
        scratch_shapes=[pltpu.VMEM((TOP_K, tm, d), F32), pltpu.SemaphoreType.DMA(())],
        compiler_params=_cparams(("arbitrary",)), name="moe_combine",
    )(dest, info, xa, gt_f, g_post.reshape(1, d), ybuf)


def _moe(hp, logits, xa, gt_f, g_post, w_gu, b_gu, w_dn, b_dn, rows_per_group, blk, tm_route, tm_move):
    n = hp.shape[0]
    info, counts = _route(logits, tm_route)
    idx = info[:, 0:TOP_K].astype(I32)
    rank = info[:, TOP_K:2 * TOP_K].astype(I32)
    cnt = counts[0, :N_EXPERTS].astype(I32)
    padded = (cnt + blk - 1) // blk * blk
    ends = jnp.cumsum(padded)
    experts = jnp.arange(N_EXPERTS, dtype=I32)
    first_row = jnp.sum(jnp.where(idx[..., None] == experts, ends - padded, 0), axis=-1)
    dest = (first_row + rank).reshape(-1)
    n_blocks = -(-n * TOP_K // blk) + N_EXPERTS
    block_row = jnp.arange(n_blocks, dtype=I32) * blk
    blk_expert = jnp.minimum(jnp.sum((ends[None, :] <= block_row[:, None]).astype(I32), axis=-1), N_EXPERTS - 1)
    xbuf = _dispatch(dest, hp, n_blocks * blk, tm_move)
    ybuf = _ffn(blk_expert, xbuf, w_gu, b_gu, w_dn, b_dn, blk)
    return _combine(dest, info, xa, gt_f, g_post, ybuf, rows_per_group, tm_move)


def _bias_of_distance(rel_bias, dist):
    onehot = (_t5_bucket(dist)[..., None] == jnp.arange(N_BUCKETS, dtype=I32)).astype(F32)
    return jnp.einsum('...k,kh->...h', onehot, rel_bias, precision=HI)


def _bias_tables(rel_bias, n_cmp):
    r = jnp.arange(QB, dtype=I32)
    d = (jnp.arange(3, dtype=I32) * QB)[:, None, None] + r[None, :, None] - r[None, None, :]
    tz = _bias_of_distance(rel_bias, d).transpose(3, 0, 1, 2)
    tz = tz.reshape(G_NSA, R_NSA, 3, QB, QB).transpose(0, 2, 1, 3, 4).reshape(G_NSA, 3, R_NSA * QB, QB)
    j = jnp.arange(n_cmp, dtype=I32)
    dc = r[:, None] - CMP_BLOCK * (j[None, :] - 4) - (CMP_BLOCK - 1)
    fc = jnp.where((j < 8)[None, :, None], _bias_of_distance(rel_bias, dc), rel_bias[N_BUCKETS - 1][None, None, :])
    return tz, fc.transpose(2, 0, 1)


def _prompt_attention(sb_q, sb_kv, nsa_q, cmp_kv, sel_kv, win_kv, gates, w_cmp_pos, rel_bias, b, t):
    n_cmp = t // CMP_BLOCK
    tz, fc = _bias_tables(rel_bias, n_cmp)
    wt = jnp.tile(w_cmp_pos, (1, n_cmp))
    o_sb = _sb_prompt(sb_q, sb_kv, b, t)
    o_cmp, imp = _cmp_prompt(nsa_q, cmp_kv, wt, fc, b, t)
    o_nsa = _selwin_prompt(nsa_q, imp, gates, o_cmp, sel_kv, win_kv, tz, b, t)
    return o_sb, o_nsa


def _prep_weights(w_in, w_out, w_router, b_router, l):
    pad_in = sum(_SPLITS) - w_in.shape[2]
    pad_r = LANES - N_EXPERTS
    return dict(w_in_b=jnp.pad(w_in[l], ((0, 0), (0, pad_in))).astype(BF16), w_out_b=w_out[l].astype(BF16),
                w_r=jnp.pad(w_router[l], ((0, 0), (0, pad_r))), b_r=jnp.pad(b_router[l], (0, pad_r)))


def _tile(n, want):
    while n % want:
        want //= 2
    return want


def _prompt_layer(x_prompt, mods, w, l):
    b, t, d = x_prompt.shape
    n = b * t
    x = x_prompt.reshape(n, d)
    sh_a, sc_a, gt_a, sh_f, sc_f, gt_f = [m.reshape(b, 1, d) for m in mods]
    sb_q, sb_kv, nsa_q, cmp_kv, sel_kv, win_kv, gates = _proj_in(
        x, w['g_attn_pre'][l], sh_a, sc_a, w['w_in_b'], t, _tile(t, 512))
    o_sb, o_nsa = _prompt_attention(sb_q, sb_kv, nsa_q, cmp_kv, sel_kv, win_kv, gates, w['w_cmp_pos'][l],
                                    w['rel_bias'], b, t)
    xa, hp, logits = _merge(x, o_sb, o_nsa, gt_a, sh_f, sc_f, w['g_sb_out'][l], w['g_nsa_out'][l], w['g_attn_post'][l],
                            w['g_ffn_pre'][l], w['w_out_b'], w['w_r'], w['b_r'], t, _tile(t, 256))
    y = _moe(hp, logits, xa, gt_f, w['g_ffn_post'][l], w['w_gu'][l], w['b_gu'][l], w['w_dn'][l], w['b_dn'][l],
             t, 512, _tile(n, 512), _tile(t, 256))
    win_buf = min(WINDOW, t)
    kv5 = lambda a, heads: a.reshape(b, t, 2, heads, HEAD_DIM)
    return (y.reshape(b, t, d), kv5(sb_kv, H_SB), kv5(cmp_kv, G_NSA), kv5(sel_kv, G_NSA),
            kv5(win_kv, G_NSA)[:, t - win_buf:])


def _sample_bias_tables(rel_bias, past, ts, n_pages, ncp, w):
    t = jnp.repeat(jnp.arange(ts, dtype=I32), H_NSA)
    h = jnp.tile(jnp.arange(H_NSA, dtype=I32), ts)
    own_head = (h[:, None] == jnp.arange(H_NSA, dtype=I32)).astype(F32)[:, None, :]
    look = lambda d: jnp.sum(_bias_of_distance(rel_bias, d) * own_head, axis=-1)
    pos = past + t[:, None]
    bias_c = look(pos - (jnp.arange(ncp, dtype=I32)[None, :] * CMP_BLOCK + CMP_BLOCK - 1))
    bias_s = look(pos - jnp.arange((n_pages + 1) * QB, dtype=I32)[None, :])
    bias_s = bias_s.reshape(ts * H_NSA, n_pages + 1, QB).transpose(1, 0, 2)
    bias_w = look(w + t[:, None] - jnp.arange(w + QB, dtype=I32)[None, :])
    return bias_c, bias_s, bias_w


def _sample_layer(x_sample, mods, caches, state_win, page_table, w, l):
    db, ts, d = x_sample.shape
    ns = db * ts
    cache_sb, cache_cmp, cache_sel = [_keys_minor(c) for c in caches]
    n_pages, page = page_table.shape[1], cache_sb.shape[-1]
    past = n_pages * page
    assert page == QB and ts <= 8
    x = x_sample.reshape(ns, d)
    tm = _tile(ns, 256)
    sh_a, sc_a, gt_a, sh_f, sc_f, gt_f = [jnp.repeat(m, ts, axis=0).reshape(ns // tm, tm, d) for m in mods]
    sb_q, sb_kv, nsa_q, cmp_kv, sel_kv, win_kv, gates = _proj_in(x, w['g_attn_pre'][l], sh_a, sc_a, w['w_in_b'], ts, tm)
    per_seq = lambda a: a.reshape(db, ts, a.shape[1])
    new_rows = lambda a: jnp.pad(per_seq(a), ((0, 0), (0, QB - ts), (0, 0)))
    pp = _tile(n_pages, 8)
    o_sb = _sb_sample(page_table, per_seq(sb_q), new_rows(sb_kv), cache_sb, l)
    total = past + ts + (-(past + ts)) % SEL_BLOCK
    ncp = -(-(total // CMP_BLOCK + 1) // LANES) * LANES
    wbuf = state_win.shape[1]
    bias_c, bias_s, bias_w = _sample_bias_tables(w['rel_bias'], past, ts, n_pages, ncp, wbuf)
    wrow = jnp.tile(w['w_cmp_pos'][l], (1, QB // CMP_BLOCK))
    nq = per_seq(nsa_q)
    o_cmp, sel32 = _cmp_sample(page_table, nq, new_rows(cmp_kv), cache_cmp, l, wrow, bias_c, past, pp)
    o_nsa = _selwin_sample(page_table, nq, new_rows(sel_kv), sel32, bias_s, o_cmp, per_seq(gates),
                           state_win.reshape(db, wbuf, 2 * D_KV), new_rows(win_kv), bias_w, cache_sel, l, pp)
    flat = lambda o: o[:, :ts].reshape(ns, o.shape[2])
    xa, hp, logits = _merge(x, flat(o_sb), flat(o_nsa), gt_a, sh_f, sc_f, w['g_sb_out'][l], w['g_nsa_out'][l],
                            w['g_attn_post'][l], w['g_ffn_pre'][l], w['w_out_b'], w['w_r'], w['b_r'], ts, tm)
    y = _moe(hp, logits, xa, gt_f, w['g_ffn_post'][l], w['w_gu'][l], w['b_gu'][l], w['w_dn'][l], w['b_dn'][l],
             ts, 128, _tile(ns, 512), tm)
    kv5 = lambda a, heads: a.reshape(db, ts, 2, heads, HEAD_DIM)
    win_new = jnp.concatenate([state_win, kv5(win_kv, G_NSA)], axis=1)[:, ts:]
    return y.reshape(db, ts, d), kv5(sb_kv, H_SB), kv5(cmp_kv, G_NSA), kv5(sel_kv, G_NSA), win_new


def kernel(x_prompt, x_sample, cache_sb_kv, cache_cmp_kv, cache_sel_kv, state_win_kv, page_table, c_prompt, c_sample, w_ada, b_ada, g_attn_pre, g_attn_post, w_in, w_cmp_pos, rel_bias, g_sb_out, g_nsa_out, w_out, g_ffn_pre, g_ffn_post, w_router, b_router, w_gu, b_gu, w_dn, b_dn):
    depth = w_in.shape[0]
    assert depth == 1
    b = x_prompt.shape[0]
    xp, xs = x_prompt, x_sample
    outs = [[] for _ in range(8)]
    for l in range(depth):
        w = _prep_weights(w_in, w_out, w_router, b_router, l)
        w.update(g_attn_pre=g_attn_pre, g_attn_post=g_attn_post, w_cmp_pos=w_cmp_pos, rel_bias=rel_bias, g_sb_out=g_sb_out,
                 g_nsa_out=g_nsa_out, g_ffn_pre=g_ffn_pre, g_ffn_post=g_ffn_post, w_gu=w_gu, b_gu=b_gu, w_dn=w_dn, b_dn=b_dn)
        mod = _adaln(jnp.concatenate([c_prompt, c_sample], axis=0), w_ada[l], b_ada[l])
        xp, sb_p, cmp_p, sel_p, win_p = _prompt_layer(xp, jnp.split(mod[:b], 6, axis=-1), w, l)
        xs, sb_s, cmp_s, sel_s, win_s = _sample_layer(
            xs, jnp.split(mod[b:], 6, axis=-1), (cache_sb_kv, cache_cmp_kv, cache_sel_kv), state_win_kv[l],
            page_table, w, l)
        for o, v in zip(outs, (sb_p, sb_s, cmp_p, cmp_s, sel_p, sel_s, win_p, win_s)):
            o.append(v)
    return (xp, xs) + tuple(jnp.stack(o) for o in outs)
```

```python
import functools
import math

import jax
import jax.numpy as jnp
from jax import lax
from jax.experimental import pallas as pl
from jax.experimental.pallas import tpu as pltpu

F32 = jnp.float32
BF16 = jnp.bfloat16
I32 = jnp.int32
HI = lax.Precision.HIGHEST

HEAD_DIM = 64
H_SB = 8
H_NSA = 8
G_NSA = 2
R_NSA = H_NSA // G_NSA
D_SB = H_SB * HEAD_DIM
D_NSA = H_NSA * HEAD_DIM
D_KV = G_NSA * HEAD_DIM
CMP_BLOCK = 32
SEL_BLOCK = 64
TOP_N = 16
WINDOW = 512
N_BUCKETS = 32
MAX_DISTANCE = 128
N_EXPERTS = 32
TOP_K = 4
SWIGLU_LIMIT = 7.0
SWIGLU_ALPHA = 1.702
EPS = 1e-6
MASK_VALUE = -1e30
SCALE = HEAD_DIM ** -0.5

LANES = 128
QB = 128
GATE_PAD = LANES
VMEM_LIMIT = 56 * 1024 * 1024


def _cparams(sem, vmem=VMEM_LIMIT):
    return pltpu.CompilerParams(dimension_semantics=sem, vmem_limit_bytes=vmem)


def _dot(a, b, precision=None):
    return jnp.dot(a, b, preferred_element_type=F32, precision=precision)


def _dot_nt(a, b, precision=None):
    return lax.dot_general(a, b, (((1,), (1,)), ((), ())), preferred_element_type=F32, precision=precision)


def _iota(shape, axis):
    return lax.broadcasted_iota(I32, shape, axis)


def _rms(x, g):
    return x * lax.rsqrt(jnp.mean(x * x, axis=-1, keepdims=True) + EPS) * g


def _t5_bucket(dist):
    n = jnp.maximum(dist, 0)
    max_exact = N_BUCKETS // 2
    log_ratio = jnp.log(jnp.maximum(n, 1).astype(F32) / max_exact) / math.log(MAX_DISTANCE / max_exact)
    large = jnp.minimum(max_exact + (log_ratio * (N_BUCKETS - max_exact)).astype(I32), N_BUCKETS - 1)
    return jnp.where(n < max_exact, n, large)


def _adaln_kernel(c_ref, w_ref, b_ref, o_ref):
    c = c_ref[...]
    o_ref[...] = _dot(c * jax.nn.sigmoid(c), w_ref[...], HI) + b_ref[...]


def _adaln(c, w, b):
    r, d = c.shape
    n = w.shape[1]
    tn = 1024
    return pl.pallas_call(
        _adaln_kernel, grid=(n // tn,),
        in_specs=[pl.BlockSpec((r, d), lambda j: (0, 0)), pl.BlockSpec((d, tn), lambda j: (0, j)),
                  pl.BlockSpec((1, tn), lambda j: (0, j))],
        out_specs=pl.BlockSpec((r, tn), lambda j: (0, j)),
        out_shape=jax.ShapeDtypeStruct((r, n), F32), compiler_params=_cparams(("arbitrary",)), name="adaln",
    )(c, w, b.reshape(1, n))


def _mod_spec(mod, tm, rows_per_group):
    g, r, d = mod.shape
    tiles = rows_per_group // tm if r == 1 else 1
    return pl.BlockSpec((1, r, d), lambda i: (i // tiles, 0, 0))


_SPLITS = (D_SB, 2 * D_SB, D_NSA, 2 * D_KV, 2 * D_KV, 2 * D_KV, GATE_PAD)


def _proj_kernel(x_ref, g_ref, sh_ref, sc_ref, w_ref, *out_refs):
    h = (_rms(x_ref[...], g_ref[...]) * (1.0 + sc_ref[0]) + sh_ref[0]).astype(BF16)
    off = 0
    for k, (ref, n) in enumerate(zip(out_refs, _SPLITS)):
        r = _dot(h, w_ref[:, off:off + n])
        ref[...] = jax.nn.sigmoid(r) if k == len(_SPLITS) - 1 else r
        off += n


def _proj_in(x, g_pre, shift, scale, w_in_b, rows_per_group, tm):
    n, d = x.shape
    nw = w_in_b.shape[1]
    row = lambda c: pl.BlockSpec((tm, c), lambda i: (i, 0))
    return pl.pallas_call(
        _proj_kernel, grid=(n // tm,),
        in_specs=[row(d), pl.BlockSpec((1, d), lambda i: (0, 0)), _mod_spec(shift, tm, rows_per_group),
                  _mod_spec(scale, tm, rows_per_group), pl.BlockSpec((d, nw), lambda i: (0, 0))],
        out_specs=[row(c) for c in _SPLITS],
        out_shape=[jax.ShapeDtypeStruct((n, c), F32) for c in _SPLITS],
        compiler_params=_cparams(("arbitrary",)), name="proj_in",
    )(x, g_pre.reshape(1, d), shift, scale, w_in_b)


def _sb_terms(z):
    l = jnp.log(1.0 + jnp.exp(-jnp.abs(z)))
    ls = jnp.minimum(z, 0.0) - l
    return ls, ls - z


def _cumsum_rhs(blk):
    r, c = _iota((blk, 2 * blk), 0), _iota((blk, 2 * blk), 1)
    return jnp.where((r > c) | (c >= blk), 1.0, 0.0).astype(BF16)


def _split_dot(x, rhs):
    hi = x.astype(BF16)
    lo = (x - hi.astype(F32)).astype(BF16)
    return _dot(hi, rhs) + _dot(lo, rhs)


def _split3_dot(x, rhs):
    hi = x.astype(BF16)
    r1 = x - hi.astype(F32)
    mid = r1.astype(BF16)
    lo = (r1 - mid.astype(F32)).astype(BF16)
    return _dot(hi, rhs) + (_dot(mid, rhs) + _dot(lo, rhs))


SB_DEAD = -106.0


def _sb_prompt_kernel(q_ref, k_ref, v_ref, o_ref, acc_s, cum_s):
    i = pl.program_id(1)
    lane = _iota((QB, LANES), 1)
    row, col = _iota((QB, QB), 0), _iota((QB, QB), 1)
    u2 = _cumsum_rhs(QB)
    acc_s[...] = jnp.zeros(acc_s.shape, F32)
    cum_s[...] = jnp.zeros(cum_s.shape, F32)

    def step(j, diag):
        rows = pl.ds(pl.multiple_of(j * QB, QB), QB)
        worst = None
        for p in range(H_SB // 2):
            cols = slice(p * LANES, (p + 1) * LANES)
            kb = k_ref[rows, cols].astype(BF16)
            vb = v_ref[rows, cols].astype(BF16)
            q = q_ref[:, cols] * SCALE
            for hf in (0, 1):
                h = 2 * p + hf
                qh = jnp.where((lane >= HEAD_DIM) == (hf == 1), q, 0.0).astype(BF16)
                ls, lk = _sb_terms(_dot_nt(qh, kb))
                if diag:
                    lk = jnp.where(col < row, lk, 0.0)
                cs = _split_dot(lk, u2)
                a = jnp.exp(ls + cs[:, :QB] + cum_s[h])
                if diag:
                    a = jnp.where(col < row, a, 0.0)
                acc_s[h] = acc_s[h] + _dot(a.astype(BF16), vb)
                cum = cum_s[h] + cs[:, QB:]
                cum_s[h] = cum
                worst = cum if worst is None else jnp.maximum(worst, cum)
        return jnp.max(worst)

    def cond(c):
        return jnp.logical_and(c[0] >= 0, c[1] > SB_DEAD)

    lax.while_loop(cond, lambda c: (c[0] - 1, step(c[0], False)), (i - 1, step(i, True)))
    for p in range(H_SB // 2):
        o_ref[:, p * LANES:(p + 1) * LANES] = jnp.where(lane < HEAD_DIM, acc_s[2 * p], acc_s[2 * p + 1])


def _sb_prompt(sb_q, sb_kv, b, t):
    nq = t // QB
    return pl.pallas_call(
        _sb_prompt_kernel, grid=(b, nq),
        in_specs=[pl.BlockSpec((QB, D_SB), lambda bi, i: (bi * nq + i, 0)),
                  pl.BlockSpec((t, D_SB), lambda bi, i: (bi, 0)),
                  pl.BlockSpec((t, D_SB), lambda bi, i: (bi, 1))],
        out_specs=pl.BlockSpec((QB, D_SB), lambda bi, i: (bi * nq + i, 0)),
        out_shape=jax.ShapeDtypeStruct((b * t, D_SB), F32),
        scratch_shapes=[pltpu.VMEM((H_SB, QB, LANES), F32)] * 2,
        compiler_params=_cparams(("arbitrary", "arbitrary")), name="sb_prompt",
    )(sb_q, sb_kv, sb_kv)


def _head_to_group_half(x2, h, lane):
    g = h // R_NSA
    if (h % 2) != g:
        x2 = pltpu.roll(x2, HEAD_DIM, axis=1)
    return jnp.where((lane >= HEAD_DIM) == (g == 1), x2, 0.0)


def _pair_tile(res, p, lane):
    g = (2 * p) // R_NSA
    a = res[2 * p] if g == 0 else pltpu.roll(res[2 * p], HEAD_DIM, axis=1)
    b = res[2 * p + 1] if g == 1 else pltpu.roll(res[2 * p + 1], HEAD_DIM, axis=1)
    return jnp.where(lane < HEAD_DIM, a, b)


def _cmp_prompt_kernel(q_ref, ckv_ref, wt_ref, fc_ref, ocmp_ref, imp_ref, ck_s, cv_s, *, t):
    i = pl.program_id(1)
    n_cmp, n_sel = t // CMP_BLOCK, t // SEL_BLOCK

    @pl.when(i == 0)
    def _():
        onb = (_iota((n_cmp, t), 1) // CMP_BLOCK) == _iota((n_cmp, t), 0)
        ck_s[...] = _dot(jnp.where(onb, wt_ref[0:1, :], 0.0), ckv_ref[:, 0:D_KV], HI)
        cv_s[...] = _dot(jnp.where(onb, wt_ref[1:2, :], 0.0), ckv_ref[:, D_KV:2 * D_KV], HI)

    lane = _iota((QB, LANES), 1)
    tpos = i * QB + _iota((QB, n_cmp), 0)
    valid = tpos >= _iota((QB, n_cmp), 1) * CMP_BLOCK + (CMP_BLOCK - 1)
    ck = ck_s[...]
    ck_hi = ck.astype(BF16)
    ck_lo = (ck - ck_hi.astype(F32)).astype(BF16)
    cvb = cv_s[...].astype(BF16)
    shift = (4 * i - 4 + n_cmp) % n_cmp
    psum = [jnp.zeros((QB, n_cmp), F32) for _ in range(G_NSA)]
    res = []
    for h in range(H_NSA):
        g, p = h // R_NSA, h // 2
        qh = _head_to_group_half(q_ref[:, p * LANES:(p + 1) * LANES], h, lane)
        q_hi = qh.astype(BF16)
        q_lo = (qh - q_hi.astype(F32)).astype(BF16)
        qk = _dot_nt(q_hi, ck_hi) + (_dot_nt(q_hi, ck_lo) + _dot_nt(q_lo, ck_hi))
        s = qk * SCALE + pltpu.roll(fc_ref[h], shift, axis=1)
        s = jnp.where(valid, s, MASK_VALUE)
        pr = jnp.where(valid, jnp.exp(s - jnp.max(s, axis=-1, keepdims=True)), 0.0)
        pr = pr / jnp.maximum(jnp.sum(pr, axis=-1, keepdims=True), 1e-30)
        psum[g] = psum[g] + pr
        res.append(_dot(pr.astype(BF16), cvb))
    for p in range(H_NSA // 2):
        ocmp_ref[:, p * LANES:(p + 1) * LANES] = _pair_tile(res, p, lane)
    pr_, pc_ = _iota((n_cmp, 2 * n_sel), 0), _iota((n_cmp, 2 * n_sel), 1)
    per = SEL_BLOCK // CMP_BLOCK
    imp = (_split3_dot(psum[0], jnp.where(pc_ == pr_ // per, 1.0, 0.0).astype(BF16))
           + _split3_dot(psum[1], jnp.where(pc_ == n_sel + pr_ // per, 1.0, 0.0).astype(BF16)))
    tq = i * QB + _iota((QB, 2 * n_sel), 0)
    j = _iota((QB, 2 * n_sel), 1) % n_sel
    imp_ref[...] = jnp.where(tq // SEL_BLOCK == j, jnp.inf, jnp.where(j * SEL_BLOCK > tq, -jnp.inf, imp))


def _cmp_prompt(nsa_q, cmp_kv, wt, fc, b, t):
    nq = t // QB
    n_cmp, n_sel = t // CMP_BLOCK, t // SEL_BLOCK
    return pl.pallas_call(
        functools.partial(_cmp_prompt_kernel, t=t), grid=(b, nq),
        in_specs=[pl.BlockSpec((QB, D_NSA), lambda bi, i: (bi * nq + i, 0)),
                  pl.BlockSpec((t, 2 * D_KV), lambda bi, i: (bi, 0)),
                  pl.BlockSpec((2, t), lambda bi, i: (0, 0)),
                  pl.BlockSpec((H_NSA, QB, n_cmp), lambda bi, i: (0, 0, 0))],
        out_specs=[pl.BlockSpec((QB, D_NSA), lambda bi, i: (bi * nq + i, 0)),
                   pl.BlockSpec((QB, 2 * n_sel), lambda bi, i: (bi * nq + i, 0))],
        out_shape=[jax.ShapeDtypeStruct((b * t, D_NSA), F32), jax.ShapeDtypeStruct((b * t, 2 * n_sel), F32)],
        scratch_shapes=[pltpu.VMEM((n_cmp, D_KV), F32), pltpu.VMEM((n_cmp, D_KV), F32)],
        compiler_params=_cparams(("arbitrary", "arbitrary")), name="nsa_cmp_prompt",
    )(nsa_q, cmp_kv, wt, fc)


def _topn_mask_t(imp_t, n_sel):
    jrow = _iota(imp_t.shape, 0) % n_sel
    rank = jnp.zeros(imp_t.shape, F32)
    w = imp_t.shape[1]
    for ii in range(n_sel):
        a = jnp.concatenate([jnp.broadcast_to(imp_t[ii:ii + 1, :], (n_sel, w)),
                             jnp.broadcast_to(imp_t[n_sel + ii:n_sel + ii + 1, :], (n_sel, w))], axis=0)
        rank = rank + jnp.where(jrow > ii, jnp.where(a >= imp_t, 1.0, 0.0), jnp.where(a > imp_t, 1.0, 0.0))
    return jnp.where(rank < min(TOP_N, n_sel), 1.0, 0.0)


def _flash_init(m_s, l_s, acc_s):
    m_s[...] = jnp.full(m_s.shape, MASK_VALUE, F32)
    l_s[...] = jnp.zeros(l_s.shape, F32)
    acc_s[...] = jnp.zeros(acc_s.shape, F32)


def _flash_update(s, vb, m_s, l_s, acc_s):
    m_old = m_s[...]
    m_new = jnp.maximum(m_old, jnp.max(s, axis=-1, keepdims=True))
    alpha = jnp.exp(m_old - m_new)
    p = jnp.exp(s - m_new)
    l_s[...] = alpha * l_s[...] + jnp.sum(p, axis=-1, keepdims=True)
    acc_s[...] = alpha * acc_s[...] + _dot(p.astype(BF16), vb)
    m_s[...] = m_new


def _flash_result(l_s, acc_s):
    return acc_s[...] / jnp.maximum(l_s[...], 1e-30)


def _gate_expand(gates, br):
    r, c = _iota((GATE_PAD, D_NSA), 0), _iota((GATE_PAD, D_NSA), 1)
    return _dot(gates, jnp.where(r == 3 * (c // HEAD_DIM) + br, 1.0, 0.0), HI)


SEL_CHUNK = 4 * QB


def _biased_masked(s, madd, tz_ref, g, i, first_block):
    out = []
    for u in range(s.shape[1] // QB):
        cs = slice(u * QB, (u + 1) * QB)
        kind = jnp.clip(i - (first_block + u), 0, 2)
        out.append(s[:, cs] + tz_ref[g, kind] + jnp.concatenate([madd[:, cs]] * R_NSA, axis=0))
    return jnp.concatenate(out, axis=1)


def _selwin_kernel(q_ref, imp_ref, gates_ref, ocmp_ref, skv_ref, wkv_ref, tz_ref, e_ref, o_ref, m_s, l_s, acc_s, mask_s, *, t):
    i = pl.program_id(1)
    n_sel = t // SEL_BLOCK
    lane = _iota((QB, LANES), 1)
    sel = _topn_mask_t(imp_ref[...].T, n_sel).T.astype(BF16)
    n_back = WINDOW // QB
    o_sel, o_win = [], []
    for g in range(G_NSA):
        qg = jnp.concatenate(
            [_head_to_group_half(q_ref[:, (h // 2) * LANES:(h // 2 + 1) * LANES] * SCALE, h, lane).astype(BF16)
             for h in range(g * R_NSA, (g + 1) * R_NSA)], axis=0)

        for c in range(t // SEL_CHUNK):
            cs = slice(c * SEL_CHUNK, (c + 1) * SEL_CHUNK)
            mask_s[:, cs] = jnp.where(_dot(sel, e_ref[g, :, cs]) > 0.5, 0.0, MASK_VALUE)
        _flash_init(m_s, l_s, acc_s)
        qpos = i * QB + _iota((QB, SEL_CHUNK), 0)

        def sel_body(c, _):
            start = pl.multiple_of(c * SEL_CHUNK, SEL_CHUNK)
            kb = skv_ref[pl.ds(start, SEL_CHUNK), 0:D_KV].astype(BF16)
            vb = skv_ref[pl.ds(start, SEL_CHUNK), D_KV:2 * D_KV].astype(BF16)
            causal = jnp.where(start + _iota((QB, SEL_CHUNK), 1) <= qpos, 0.0, MASK_VALUE)
            s = _biased_masked(_dot_nt(qg, kb), mask_s[:, pl.ds(start, SEL_CHUNK)] + causal, tz_ref, g, i,
                               c * (SEL_CHUNK // QB))
            m_old = m_s[...]
            m_new = jnp.maximum(m_old, jnp.max(s, axis=-1, keepdims=True))
            alpha = jnp.exp(m_old - m_new)
            p = jnp.exp(s - jnp.concatenate([m_new] * (SEL_CHUNK // LANES), axis=1))
            l_s[...] = alpha * l_s[...] + jnp.sum(p, axis=-1, keepdims=True)
            acc_s[...] = alpha * acc_s[...] + _dot(p.astype(BF16), vb)
            m_s[...] = m_new
            return 0

        lax.fori_loop(0, (i * QB + QB + SEL_CHUNK - 1) // SEL_CHUNK, sel_body, 0)
        o_sel.append(_flash_result(l_s, acc_s))

        first = jnp.maximum(i - n_back, 0)
        wk = (n_back + 1) * QB
        start = pl.multiple_of(first * QB, QB)
        kb = wkv_ref[pl.ds(start, wk), 0:D_KV].astype(BF16)
        vb = wkv_ref[pl.ds(start, wk), D_KV:2 * D_KV].astype(BF16)
        dist = i * QB + _iota((QB, wk), 0) - (start + _iota((QB, wk), 1))
        inside = jnp.where(dist >= 0, jnp.where(dist < WINDOW, 0.0, MASK_VALUE), MASK_VALUE)
        s = _biased_masked(_dot_nt(qg, kb), inside, tz_ref, g, i, first)
        p = jnp.exp(s - jnp.max(s, axis=-1, keepdims=True))
        o_win.append(_dot(p.astype(BF16), vb) / jnp.maximum(jnp.sum(p, axis=-1, keepdims=True), 1e-30))

    def heads(o):
        return [o[h // R_NSA][(h % R_NSA) * QB:(h % R_NSA + 1) * QB, :] for h in range(H_NSA)]

    hs, hw = heads(o_sel), heads(o_win)
    gates = gates_ref[...]
    g_cmp, g_sel, g_win = [_gate_expand(gates, br) for br in range(3)]
    for p in range(H_NSA // 2):
        cs = slice(p * LANES, (p + 1) * LANES)
        o_ref[:, cs] = (g_cmp[:, cs] * ocmp_ref[:, cs] + g_sel[:, cs] * _pair_tile(hs, p, lane)
                        + g_win[:, cs] * _pair_tile(hw, p, lane))


def _selwin_prompt(nsa_q, imp, gates, o_cmp, sel_kv, win_kv, tz, b, t):
    nq = t // QB
    n_sel = t // SEL_BLOCK
    assert t % SEL_CHUNK == 0 and t >= WINDOW + QB
    rowspec = lambda c: pl.BlockSpec((QB, c), lambda bi, i: (bi * nq + i, 0))
    seqspec = pl.BlockSpec((t, 2 * D_KV), lambda bi, i: (bi, 0))
    key_block = jnp.arange(t, dtype=I32) // SEL_BLOCK
    expand = (jnp.arange(2 * n_sel, dtype=I32)[None, :, None]
              == jnp.arange(G_NSA, dtype=I32)[:, None, None] * n_sel + key_block[None, None, :]).astype(BF16)
    return pl.pallas_call(
        functools.partial(_selwin_kernel, t=t), grid=(b, nq),
        in_specs=[rowspec(D_NSA), rowspec(2 * n_sel), rowspec(GATE_PAD), rowspec(D_NSA), seqspec, seqspec,
                  pl.BlockSpec((G_NSA, 3, R_NSA * QB, QB), lambda bi, i: (0, 0, 0, 0)),
                  pl.BlockSpec((G_NSA, 2 * n_sel, t), lambda bi, i: (0, 0, 0))],
        out_specs=rowspec(D_NSA),
        out_shape=jax.ShapeDtypeStruct((b * t, D_NSA), F32),
        scratch_shapes=[pltpu.VMEM((R_NSA * QB, LANES), F32)] * 3 + [pltpu.VMEM((QB, t), F32)],
        compiler_params=_cparams(("arbitrary", "arbitrary")), name="nsa_selwin_prompt",
    )(nsa_q, imp, gates, o_cmp, sel_kv, win_kv, tz, expand)


def _head_rows(q, ts):
    rows = ts * H_SB
    qs = jnp.concatenate([jnp.broadcast_to(q[t:t + 1], (H_SB, q.shape[1])) for t in range(ts)], axis=0)
    return jnp.where(_iota((rows, q.shape[1]), 1) // HEAD_DIM == _iota((rows, q.shape[1]), 0) % H_SB, qs, 0.0)


def _collapse_rows(x, ts):
    rows = ts * H_SB
    x = jnp.where(_iota((rows, x.shape[1]), 1) // HEAD_DIM == _iota((rows, x.shape[1]), 0) % H_SB, x, 0.0)
    out = [jnp.sum(x[t * H_SB:(t + 1) * H_SB], axis=0, keepdims=True) for t in range(ts)]
    return jnp.concatenate(out + [jnp.zeros((8 - ts, x.shape[1]), F32)], axis=0)


def _fold_q(q, ts):
    fr, fc = _iota((D_NSA, LANES), 0), _iota((D_NSA, LANES), 1)
    fold = jnp.where(fc == ((fr // HEAD_DIM) // R_NSA) * HEAD_DIM + fr % HEAD_DIM, 1.0, 0.0)
    return _dot(_head_rows(q, ts), fold, HI)


def _unfold_rows(o, ts):
    ur, uc = _iota((LANES, D_NSA), 0), _iota((LANES, D_NSA), 1)
    unf = jnp.where(ur % HEAD_DIM == uc % HEAD_DIM, jnp.where(ur // HEAD_DIM == (uc // HEAD_DIM) // R_NSA, 1.0, 0.0), 0.0)
    return _collapse_rows(_dot(o, unf, HI), ts)


def _keys_minor(cache):
    return jnp.transpose(cache, (0, 1, 3, 4, 5, 2))


def _sb_sample_kernel(pt_ref, q_ref, new_ref, cache_ref, o_ref, buf, sem, cum_s, acc_s, *, ts, n_pages, layer):
    b = pl.program_id(0)
    rows = ts * H_SB
    u2 = _cumsum_rhs(QB)

    def page_copy(k, slot):
        return pltpu.make_async_copy(cache_ref.at[layer, pt_ref[b, n_pages - 1 - k]], buf.at[slot], sem.at[slot])

    page_copy(0, 0).start()

    qs = _head_rows(q_ref[0] * SCALE, ts).astype(BF16)
    new = new_ref[0]
    valid = _iota((rows, QB), 1) < _iota((rows, QB), 0) // H_SB
    ls, lk = _sb_terms(_dot_nt(qs, new[:, 0:D_SB].astype(BF16)))
    cs = _split_dot(jnp.where(valid, lk, 0.0), u2)
    a = jnp.where(valid, jnp.exp(ls + cs[:, :QB]), 0.0)
    acc_s[...] = _dot(a.astype(BF16), new[:, D_SB:2 * D_SB].astype(BF16))
    cum_s[...] = cs[:, QB:]

    def walk(c):
        k, _ = c
        slot = k % 2
        page_copy(k, slot).wait()

        @pl.when(k + 1 < n_pages)
        def _():
            page_copy(k + 1, 1 - slot).start()

        kt = buf[slot, 0].reshape(D_SB, QB).astype(BF16)
        vt = buf[slot, 1].reshape(D_SB, QB).astype(BF16)
        ls, lk = _sb_terms(_dot(qs, kt))
        cs = _split_dot(lk, u2)
        a = jnp.exp(ls + cs[:, :QB] + cum_s[...])
        acc_s[...] = acc_s[...] + _dot_nt(a.astype(BF16), vt)
        cum = cum_s[...] + cs[:, QB:]
        cum_s[...] = cum
        return k + 1, jnp.max(cum)

    k_end, _ = lax.while_loop(lambda c: jnp.logical_and(c[0] < n_pages, c[1] > SB_DEAD), walk,
                              (jnp.int32(0), jnp.max(cum_s[...])))

    @pl.when(k_end < n_pages)
    def _():
        page_copy(k_end, k_end % 2).wait()

    o_ref[0] = _collapse_rows(acc_s[...], ts)


def _paged_specs(cache, layer, pp):
    def spec(ip):
        return pl.BlockSpec((1, 1) + tuple(cache.shape[2:]), lambda b, c, pt: (layer, pt[b, c * pp + ip], 0, 0, 0, 0))
    return [spec(ip) for ip in range(pp)]


def _page_kt_vt(page_ref):
    return page_ref[0, 0, 0].reshape(D_KV, QB), page_ref[0, 0, 1].reshape(D_KV, QB)


def _batch_spec(shape):
    return pl.BlockSpec((1,) + tuple(shape[1:]), lambda b, c, pt: (b,) + (0,) * (len(shape) - 1))


def _const_spec(shape):
    return pl.BlockSpec(tuple(shape), lambda b, c, pt: (0,) * len(shape))


def _sb_sample(page_table, q, new_kv, cache, layer):
    db, ts, _ = q.shape
    n_pages = page_table.shape[1]
    rows = ts * H_SB
    per_seq = lambda shape: pl.BlockSpec((1,) + tuple(shape[1:]), lambda b, pt: (b,) + (0,) * (len(shape) - 1))
    grid_spec = pltpu.PrefetchScalarGridSpec(
        num_scalar_prefetch=1, grid=(db,),
        in_specs=[per_seq(q.shape), per_seq(new_kv.shape), pl.BlockSpec(memory_space=pl.ANY)],
        out_specs=per_seq((db, 8, D_SB)),
        scratch_shapes=[pltpu.VMEM((2,) + tuple(cache.shape[2:]), F32), pltpu.SemaphoreType.DMA((2,)),
                        pltpu.VMEM((rows, QB), F32), pltpu.VMEM((rows, D_SB), F32)])
    return pl.pallas_call(
        functools.partial(_sb_sample_kernel, ts=ts, n_pages=n_pages, layer=layer), grid_spec=grid_spec,
        out_shape=jax.ShapeDtypeStruct((db, 8, D_SB), F32),
        compiler_params=_cparams(("arbitrary",)), name="sb_sample",
    )(page_table, q, new_kv, cache)


def _cmp_sample_kernel(pt_ref, q_ref, new_ref, wrow_ref, bias_ref, *rest, pp, ts, past, n_cmp, n_sel):
    del pt_ref
    caches, (ocmp_ref, sel_ref), (ck_s, cv_s) = rest[:pp], rest[pp:pp + 2], rest[pp + 2:]
    c, nc = pl.program_id(1), pl.num_programs(1)
    ncp, nsp = ck_s.shape[0], sel_ref.shape[2]
    rows = ts * H_NSA
    per_page = QB // CMP_BLOCK

    @pl.when(c == 0)
    def _():
        ck_s[...] = jnp.zeros(ck_s.shape, F32)
        cv_s[...] = jnp.zeros(cv_s.shape, F32)

    nrow, rcol = _iota((ncp, QB), 0), _iota((ncp, QB), 1)

    def add_rows(kv, first_block):
        on = nrow == jnp.minimum(first_block + rcol // CMP_BLOCK, n_cmp)
        ck_s[...] = ck_s[...] + _dot(jnp.where(on, wrow_ref[0:1, :], 0.0), kv[:, 0:D_KV], HI)
        cv_s[...] = cv_s[...] + _dot(jnp.where(on, wrow_ref[1:2, :], 0.0), kv[:, D_KV:2 * D_KV], HI)

    orow, ocol = _iota((pp * per_page, QB), 0), _iota((pp * per_page, QB), 1)
    ck_new = jnp.zeros((pp * per_page, D_KV), F32)
    cv_new = jnp.zeros((pp * per_page, D_KV), F32)
    for ip in range(pp):
        on = jnp.where(orow == ip * per_page + ocol // CMP_BLOCK, 1.0, 0.0).astype(BF16)
        kt, vt = _page_kt_vt(caches[ip])
        for w_row, src, is_k in ((wrow_ref[0:1, :], kt, True), (wrow_ref[1:2, :], vt, False)):
            x = src * w_row
            hi = x.astype(BF16)
            lo = (x - hi.astype(F32)).astype(BF16)
            add = _dot_nt(on, hi) + _dot_nt(on, lo)
            if is_k:
                ck_new = ck_new + add
            else:
                cv_new = cv_new + add
    out_rows = pl.ds(pl.multiple_of(c * (pp * per_page), pp * per_page), pp * per_page)
    ck_s[out_rows, :] = ck_new
    cv_s[out_rows, :] = cv_new

    @pl.when(c == nc - 1)
    def _():
        add_rows(new_ref[0], past // CMP_BLOCK)
        qc = _fold_q(q_ref[0], ts)
        ncol, tq = _iota((rows, ncp), 1), _iota((rows, ncp), 0) // H_NSA
        valid = ncol * CMP_BLOCK + (CMP_BLOCK - 1) <= past + tq
        s = jnp.where(valid, _dot_nt(qc, ck_s[...], HI) * SCALE + bias_ref[...], MASK_VALUE)
        pr = jnp.where(valid, jnp.exp(s - jnp.max(s, axis=-1, keepdims=True)), 0.0)
        pr = pr / jnp.maximum(jnp.sum(pr, axis=-1, keepdims=True), 1e-30)
        ocmp_ref[0] = _unfold_rows(_dot(pr.astype(BF16), cv_s[...].astype(BF16)), ts)
        gr, gc = _iota((QB, rows), 0), _iota((QB, rows), 1)
        gsum = jnp.where(gr < G_NSA * ts, jnp.where(gc == (gr % ts) * H_NSA + (gr // ts) * R_NSA + gc % R_NSA, 1.0, 0.0), 0.0)
        pr_, pc_ = _iota((ncp, nsp), 0), _iota((ncp, nsp), 1)
        imp = _dot(_dot(gsum, pr, HI), jnp.where(pc_ == pr_ // (SEL_BLOCK // CMP_BLOCK), 1.0, 0.0), HI)
        jj, tt = _iota((QB, nsp), 1), past + _iota((QB, nsp), 0) % ts
        imp = jnp.where(tt // SEL_BLOCK == jj, jnp.inf, jnp.where(jj * SEL_BLOCK > tt, -jnp.inf, imp))
        imp_t = imp.T
        ii, j2 = _iota((nsp, nsp), 0), _iota((nsp, nsp), 1)
        picks = []
        for r in range(G_NSA * ts):
            a = jnp.broadcast_to(imp_t[:, r:r + 1], (nsp, nsp))
            bb = jnp.broadcast_to(imp[r:r + 1, :], (nsp, nsp))
            beats = jnp.where(ii < j2, jnp.where(a >= bb, 1.0, 0.0), jnp.where(a > bb, 1.0, 0.0))
            picks.append(jnp.where(jnp.sum(beats, axis=0, keepdims=True) < min(TOP_N, n_sel), 1.0, 0.0))
        sel8 = jnp.concatenate(picks + [jnp.zeros((QB - G_NSA * ts, nsp), F32)], axis=0).astype(BF16)
        er, ec = _iota((rows, QB), 0), _iota((rows, QB), 1)
        expand = jnp.where(ec == ((er % H_NSA) // R_NSA) * ts + er // H_NSA, 1.0, 0.0).astype(BF16)
        sel_ref[0] = _dot(expand, sel8)


def _cmp_sample(page_table, q, new_kv, cache, layer, wrow, bias_c, past, pp):
    db, ts, _ = q.shape
    n_pages = page_table.shape[1]
    nc = n_pages // pp
    total = past + ts + (-(past + ts)) % SEL_BLOCK
    n_cmp, n_sel = total // CMP_BLOCK, total // SEL_BLOCK
    ncp = bias_c.shape[1]
    nsp = -(-n_sel // LANES) * LANES
    grid_spec = pltpu.PrefetchScalarGridSpec(
        num_scalar_prefetch=1, grid=(db, nc),
        in_specs=[_batch_spec(q.shape), _batch_spec(new_kv.shape), _const_spec(wrow.shape), _const_spec(bias_c.shape)]
        + _paged_specs(cache, layer, pp),
        out_specs=[_batch_spec((db, 8, D_NSA)), _batch_spec((db, ts * H_NSA, nsp))],
        scratch_shapes=[pltpu.VMEM((ncp, D_KV), F32), pltpu.VMEM((ncp, D_KV), F32)])
    return pl.pallas_call(
        functools.partial(_cmp_sample_kernel, pp=pp, ts=ts, past=past, n_cmp=n_cmp, n_sel=n_sel), grid_spec=grid_spec,
        out_shape=[jax.ShapeDtypeStruct((db, 8, D_NSA), F32), jax.ShapeDtypeStruct((db, ts * H_NSA, nsp), F32)],
        compiler_params=_cparams(("arbitrary", "arbitrary")), name="nsa_cmp_sample",
    )(page_table, q, new_kv, wrow, bias_c, *([cache] * pp))


def _selwin_sample_kernel(pt_ref, q_ref, newsel_ref, sel_ref, bias_ref, ocmp_ref, gates_ref, win_ref, newwin_ref,
                          biasw_ref, *rest, pp, ts, n_pages):
    del pt_ref
    caches, o_ref, (qc_s, m_s, l_s, acc_s) = rest[:pp], rest[pp], rest[pp + 1:]
    c, nc = pl.program_id(1), pl.num_programs(1)
    rows = ts * H_NSA
    nsp = sel_ref.shape[2]
    per_page = QB // SEL_BLOCK

    @pl.when(c == 0)
    def _():
        qc_s[...] = _fold_q(q_ref[0] * SCALE, ts).astype(BF16)
        _flash_init(m_s, l_s, acc_s)

    sel = sel_ref[0].astype(BF16)
    er, ec = _iota((nsp, QB), 0), _iota((nsp, QB), 1) // SEL_BLOCK
    kr, tr = _iota((rows, QB), 1), _iota((rows, QB), 0) // H_NSA

    def block(kv, page, allowed):
        picked = _dot(sel, jnp.where(er == per_page * page + ec, 1.0, 0.0).astype(BF16)) > 0.5
        s = _dot_nt(qc_s[...], kv[:, 0:D_KV].astype(BF16)) + bias_ref[page] + jnp.where(picked, allowed, MASK_VALUE)
        _flash_update(s, kv[:, D_KV:2 * D_KV].astype(BF16), m_s, l_s, acc_s)

    wide = pp * QB
    wr, wc = _iota((nsp, wide), 0), _iota((nsp, wide), 1) // SEL_BLOCK
    picked = _dot(sel, jnp.where(wr == per_page * pp * c + wc, 1.0, 0.0).astype(BF16)) > 0.5
    pages = [_page_kt_vt(caches[ip]) for ip in range(pp)]
    qc = qc_s[...]
    s = jnp.concatenate([_dot(qc, pages[ip][0].astype(BF16)) + bias_ref[c * pp + ip] for ip in range(pp)], axis=1)
    s = s + jnp.where(picked, 0.0, MASK_VALUE)
    m_old = m_s[...]
    m_new = jnp.maximum(m_old, jnp.max(s, axis=-1, keepdims=True))
    alpha = jnp.exp(m_old - m_new)
    p = jnp.exp(s - jnp.concatenate([m_new] * pp, axis=1))
    pv = sum(_dot_nt(p[:, ip * QB:(ip + 1) * QB].astype(BF16), pages[ip][1].astype(BF16)) for ip in range(pp))
    l_s[...] = alpha * l_s[...] + jnp.sum(p, axis=-1, keepdims=True)
    acc_s[...] = alpha * acc_s[...] + pv
    m_s[...] = m_new

    @pl.when(c == nc - 1)
    def _():
        new_allowed = jnp.where(kr <= tr, 0.0, MASK_VALUE)
        block(newsel_ref[0], n_pages, new_allowed)
        o_sel = _flash_result(l_s, acc_s)
        _flash_init(m_s, l_s, acc_s)
        w = win_ref.shape[1]
        for wb in range(w // QB):
            kv = win_ref[0, wb * QB:(wb + 1) * QB, :]
            s = _dot_nt(qc_s[...], kv[:, 0:D_KV].astype(BF16)) + biasw_ref[:, wb * QB:(wb + 1) * QB]
            if wb * QB <= w - WINDOW + ts - 1:
                s = s + jnp.where(wb * QB + kr > w - WINDOW + tr, 0.0, MASK_VALUE)
            _flash_update(s, kv[:, D_KV:2 * D_KV].astype(BF16), m_s, l_s, acc_s)
        kv = newwin_ref[0]
        s = _dot_nt(qc_s[...], kv[:, 0:D_KV].astype(BF16)) + biasw_ref[:, w:w + QB] + new_allowed
        _flash_update(s, kv[:, D_KV:2 * D_KV].astype(BF16), m_s, l_s, acc_s)
        o_win = _flash_result(l_s, acc_s)
        gates = jnp.concatenate([gates_ref[0], jnp.zeros((8 - ts, GATE_PAD), F32)], axis=0)
        o_ref[0] = (_gate_expand(gates, 0) * ocmp_ref[0] + _gate_expand(gates, 1) * _unfold_rows(o_sel, ts)
                    + _gate_expand(gates, 2) * _unfold_rows(o_win, ts))


def _selwin_sample(page_table, q, new_sel, sel32, bias_s, o_cmp, gates, state_win, new_win, bias_w, cache, layer, pp):
    db, ts, _ = q.shape
    n_pages = page_table.shape[1]
    nc = n_pages // pp
    rows = ts * H_NSA
    ins = [q, new_sel, sel32, bias_s, o_cmp, gates, state_win, new_win, bias_w]
    specs = [_batch_spec(q.shape), _batch_spec(new_sel.shape), _batch_spec(sel32.shape), _const_spec(bias_s.shape),
             _batch_spec(o_cmp.shape), _batch_spec(gates.shape), _batch_spec(state_win.shape), _batch_spec(new_win.shape),
             _const_spec(bias_w.shape)]
    grid_spec = pltpu.PrefetchScalarGridSpec(
        num_scalar_prefetch=1, grid=(db, nc),
        in_specs=specs + _paged_specs(cache, layer, pp),
        out_specs=_batch_spec((db, 8, D_NSA)),
        scratch_shapes=[pltpu.VMEM((rows, LANES), BF16)] + [pltpu.VMEM((rows, LANES), F32)] * 3)
    return pl.pallas_call(
        functools.partial(_selwin_sample_kernel, pp=pp, ts=ts, n_pages=n_pages), grid_spec=grid_spec,
        out_shape=jax.ShapeDtypeStruct((db, 8, D_NSA), F32),
        compiler_params=_cparams(("arbitrary", "arbitrary")), name="nsa_selwin_sample",
    )(page_table, *ins, *([cache] * pp))


def _merge_kernel(x_ref, osb_ref, onsa_ref, gt_ref, shf_ref, scf_ref, gsb_ref, gnsa_ref, gpost_ref, gpre_ref,
                  wout_ref, wr_ref, br_ref, xa_ref, hp_ref, lg_ref):
    a = _rms(osb_ref[...], gsb_ref[...]).astype(BF16)
    b = _rms(onsa_ref[...], gnsa_ref[...]).astype(BF16)
    y = _dot(a, wout_ref[0:D_SB, :]) + _dot(b, wout_ref[D_SB:D_SB + D_NSA, :])
    xa = x_ref[...] + gt_ref[0] * _rms(y, gpost_ref[...])
    xa_ref[...] = xa
    h = _rms(xa, gpre_ref[...]) * (1.0 + scf_ref[0]) + shf_ref[0]
    lg_ref[...] = _dot(h, wr_ref[...], HI) + br_ref[...]
    half = h.shape[1] // 2
    hb = pltpu.bitcast(h.astype(BF16).astype(F32), jnp.uint32)
    hp_ref[...] = hb[:, :half] | (hb[:, half:] >> 16)


def _merge(x, o_sb, o_nsa, gt_a, sh_f, sc_f, g_sb, g_nsa, g_post, g_pre, w_out_b, w_r, b_r, rows_per_group, tm):
    n, d = x.shape
    row = lambda c: pl.BlockSpec((tm, c), lambda i: (i, 0))
    full = lambda a: pl.BlockSpec(a.shape, lambda i: (0,) * a.ndim)
    vec = lambda v: v.reshape(1, -1)
    consts = [vec(g_sb), vec(g_nsa), vec(g_post), vec(g_pre), w_out_b, w_r, vec(b_r)]
    return pl.pallas_call(
        _merge_kernel, grid=(n // tm,),
        in_specs=[row(d), row(D_SB), row(D_NSA)] + [_mod_spec(m, tm, rows_per_group) for m in (gt_a, sh_f, sc_f)]
        + [full(a) for a in consts],
        out_specs=[row(d), row(d // 2), row(LANES)],
        out_shape=[jax.ShapeDtypeStruct((n, d), F32), jax.ShapeDtypeStruct((n, d // 2), jnp.uint32),
                   jax.ShapeDtypeStruct((n, LANES), F32)],
        compiler_params=_cparams(("arbitrary",)), name="merge_heads",
    )(x, o_sb, o_nsa, gt_a, sh_f, sc_f, *consts)


def _route_kernel(lg_ref, info_ref, cnt_ref, carry_s, *, tm):
    i = pl.program_id(0)

    @pl.when(i == 0)
    def _():
        carry_s[...] = jnp.zeros(carry_s.shape, F32)

    lane = _iota((tm, LANES), 1)
    l = jnp.where(lane < N_EXPERTS, lg_ref[...], -jnp.inf)
    tops, idxs, ohs = [], [], []
    for _ in range(TOP_K):
        m = jnp.max(l, axis=-1, keepdims=True)
        idx = jnp.min(jnp.where(l == m, lane, LANES), axis=-1, keepdims=True)
        oh = lane == idx
        l = jnp.where(oh, -jnp.inf, l)
        tops.append(m)
        idxs.append(idx)
        ohs.append(oh)
    e = [jnp.exp(v - tops[0]) for v in tops]
    den = e[0] + e[1] + e[2] + e[3]
    cnt = sum(jnp.where(oh, 1.0, 0.0) for oh in ohs)
    ltri = jnp.where(_iota((tm, tm), 0) > _iota((tm, tm), 1), 1.0, 0.0).astype(BF16)
    prefix = _dot(ltri, cnt.astype(BF16)) + carry_s[0:1, :]
    info = jnp.zeros((tm, LANES), F32)
    for k in range(TOP_K):
        rank = jnp.sum(jnp.where(ohs[k], prefix, 0.0), axis=-1, keepdims=True)
        info = jnp.where(lane == k, idxs[k].astype(F32), info)
        info = jnp.where(lane == TOP_K + k, rank, info)
        info = jnp.where(lane == 2 * TOP_K + k, e[k] / den, info)
    info_ref[...] = info
    carry_s[...] = carry_s[...] + jnp.sum(cnt, axis=0, keepdims=True)
    cnt_ref[...] = carry_s[...]


def _route(logits, tm):
    n = logits.shape[0]
    return pl.pallas_call(
        functools.partial(_route_kernel, tm=tm), grid=(n // tm,),
        in_specs=[pl.BlockSpec((tm, LANES), lambda i: (i, 0))],
        out_specs=[pl.BlockSpec((tm, LANES), lambda i: (i, 0)), pl.BlockSpec((8, LANES), lambda i: (0, 0))],
        out_shape=[jax.ShapeDtypeStruct((n, LANES), F32), jax.ShapeDtypeStruct((8, LANES), F32)],
        scratch_shapes=[pltpu.VMEM((8, LANES), F32)],
        compiler_params=_cparams(("arbitrary",)), name="moe_route",
    )(logits)


def _dispatch_kernel(dest_ref, hp_ref, xz_ref, xbuf_ref, sem, *, tm):
    del xz_ref

    def row_copy(t, d):
        return pltpu.make_async_copy(hp_ref.at[pl.ds(t, 1), :], xbuf_ref.at[pl.ds(d, 1), :], sem)

    def start(t, _):
        for k in range(TOP_K):
            row_copy(t, dest_ref[t * TOP_K + k]).start(priority=k % 2)
        return 0

    def wait(t, _):
        for k in range(TOP_K):
            row_copy(t, dest_ref[t * TOP_K + k]).wait()
        return 0

    lax.fori_loop(0, tm, start, 0)
    lax.fori_loop(0, tm, wait, 0)


def _dispatch(dest, hp, n_rows, tm):
    n, w = hp.shape
    xz = jnp.zeros((n_rows, w), hp.dtype)
    return pl.pallas_call(
        functools.partial(_dispatch_kernel, tm=tm), grid=(n // tm,),
        in_specs=[pl.BlockSpec((tm * TOP_K,), lambda i: (i,), memory_space=pltpu.SMEM),
                  pl.BlockSpec((tm, w), lambda i: (i, 0)), pl.BlockSpec(memory_space=pl.ANY)],
        out_specs=pl.BlockSpec(memory_space=pl.ANY),
        out_shape=jax.ShapeDtypeStruct((n_rows, w), hp.dtype),
        scratch_shapes=[pltpu.SemaphoreType.DMA(())],
        input_output_aliases={2: 0},
        compiler_params=_cparams(("arbitrary",)), name="moe_dispatch",
    )(dest, hp, xz)


def _ffn_kernel(be_ref, x_ref, wgu_ref, bgu_ref, wdn_ref, bdn_ref, y_ref, wgu_s, wdn_s):
    b = pl.program_id(0)
    changed = jnp.logical_or(b == 0, be_ref[b] != be_ref[jnp.maximum(b - 1, 0)])

    @pl.when(changed)
    def _():
        wgu_s[...] = wgu_ref[0].astype(BF16)
        wdn_s[...] = wdn_ref[0].astype(BF16)

    xw = x_ref[...]
    half = xw.shape[1]
    xa = pltpu.bitcast(xw & jnp.uint32(0xFFFF0000), F32).astype(BF16)
    xb = pltpu.bitcast(xw << 16, F32).astype(BF16)
    gu = _dot(xa, wgu_s[0:half, :]) + _dot(xb, wgu_s[half:2 * half, :]) + bgu_ref[0]
    dff = gu.shape[1] // 2
    glu = jnp.minimum(gu[:, :dff], SWIGLU_LIMIT)
    lin = jnp.clip(gu[:, dff:], -SWIGLU_LIMIT, SWIGLU_LIMIT)
    act = glu * jax.nn.sigmoid(SWIGLU_ALPHA * glu) * (lin + 1.0)
    y_ref[...] = _dot(act.astype(BF16), wdn_s[...]) + bdn_ref[0]


def _ffn(blk_expert, xbuf, w_gu, b_gu, w_dn, b_dn, blk):
    n_rows, half = xbuf.shape
    ne, d, dff2 = w_gu.shape
    dff = dff2 // 2
    grid_spec = pltpu.PrefetchScalarGridSpec(
        num_scalar_prefetch=1, grid=(n_rows // blk,),
        in_specs=[pl.BlockSpec((blk, half), lambda b, be: (b, 0)),
                  pl.BlockSpec((1, d, dff2), lambda b, be: (be[b], 0, 0)),
                  pl.BlockSpec((1, 1, dff2), lambda b, be: (be[b], 0, 0)),
                  pl.BlockSpec((1, dff, d), lambda b, be: (be[b], 0, 0)),
                  pl.BlockSpec((1, 1, d), lambda b, be: (be[b], 0, 0))],
        out_specs=pl.BlockSpec((blk, d), lambda b, be: (b, 0)),
        scratch_shapes=[pltpu.VMEM((d, dff2), BF16), pltpu.VMEM((dff, d), BF16)])
    return pl.pallas_call(
        _ffn_kernel, grid_spec=grid_spec, out_shape=jax.ShapeDtypeStruct((n_rows, d), F32),
        compiler_params=_cparams(("arbitrary",)), name="moe_ffn",
    )(blk_expert, xbuf, w_gu, b_gu.reshape(ne, 1, dff2), w_dn, b_dn.reshape(ne, 1, d))


def _combine_kernel(dest_ref, info_ref, xa_ref, gt_ref, gpost_ref, ybuf_ref, o_ref, rows_s, sem, *, tm):
    def row_copy(t, k):
        return pltpu.make_async_copy(ybuf_ref.at[pl.ds(dest_ref[t * TOP_K + k], 1), :], rows_s.at[k, pl.ds(t, 1), :], sem)

    def start(t, _):
        for k in range(TOP_K):
            row_copy(t, k).start(priority=k % 2)
        return 0

    def wait(t, _):
        for k in range(TOP_K):
            row_copy(t, k).wait()
        return 0

    lax.fori_loop(0, tm, start, 0)
    lax.fori_loop(0, tm, wait, 0)
    info = info_ref[...]
    y = sum(info[:, 2 * TOP_K + k:2 * TOP_K + k + 1] * rows_s[k] for k in range(TOP_K))
    o_ref[...] = xa_ref[...] + gt_ref[0] * _rms(y, gpost_ref[...])


def _combine(dest, info, xa, gt_f, g_post, ybuf, rows_per_group, tm):
    n, d = xa.shape
    row = lambda c: pl.BlockSpec((tm, c), lambda i: (i, 0))
    return pl.pallas_call(
        functools.partial(_combine_kernel, tm=tm), grid=(n // tm,),
        in_specs=[pl.BlockSpec((tm * TOP_K,), lambda i: (i,), memory_space=pltpu.SMEM), row(LANES), row(d),
                  _mod_spec(gt_f, tm, rows_per_group), pl.BlockSpec((1, d), lambda i: (0, 0)),
                  pl.BlockSpec(memory_space=pl.ANY)],
        out_specs=row(d),
        out_shape=jax.ShapeDtypeStruct((n, d), F32),
        scratch_shapes=[pltpu.VMEM((TOP_K, tm, d), F32), pltpu.SemaphoreType.DMA(())],
        compiler_params=_cparams(("arbitrary",)), name="moe_combine",
    )(dest, info, xa, gt_f, g_post.reshape(1, d), ybuf)


def _moe(hp, logits, xa, gt_f, g_post, w_gu, b_gu, w_dn, b_dn, rows_per_group, blk, tm_route, tm_move):
    n = hp.shape[0]
    info, counts = _route(logits, tm_route)
    idx = info[:, 0:TOP_K].astype(I32)
    rank = info[:, TOP_K:2 * TOP_K].astype(I32)
    cnt = counts[0, :N_EXPERTS].astype(I32)
    padded = (cnt + blk - 1) // blk * blk
    ends = jnp.cumsum(padded)
    experts = jnp.arange(N_EXPERTS, dtype=I32)
    first_row = jnp.sum(jnp.where(idx[..., None] == experts, ends - padded, 0), axis=-1)
    dest = (first_row + rank).reshape(-1)
    n_blocks = -(-n * TOP_K // blk) + N_EXPERTS
    block_row = jnp.arange(n_blocks, dtype=I32) * blk
    blk_expert = jnp.minimum(jnp.sum((ends[None, :] <= block_row[:, None]).astype(I32), axis=-1), N_EXPERTS - 1)
    xbuf = _dispatch(dest, hp, n_blocks * blk, tm_move)
    ybuf = _ffn(blk_expert, xbuf, w_gu, b_gu, w_dn, b_dn, blk)
    return _combine(dest, info, xa, gt_f, g_post, ybuf, rows_per_group, tm_move)


def _bias_of_distance(rel_bias, dist):
    onehot = (_t5_bucket(dist)[..., None] == jnp.arange(N_BUCKETS, dtype=I32)).astype(F32)
    return jnp.einsum('...k,kh->...h', onehot, rel_bias, precision=HI)


def _bias_tables(rel_bias, n_cmp):
    r = jnp.arange(QB, dtype=I32)
    d = (jnp.arange(3, dtype=I32) * QB)[:, None, None] + r[None, :, None] - r[None, None, :]
    tz = _bias_of_distance(rel_bias, d).transpose(3, 0, 1, 2)
    tz = tz.reshape(G_NSA, R_NSA, 3, QB, QB).transpose(0, 2, 1, 3, 4).reshape(G_NSA, 3, R_NSA * QB, QB)
    j = jnp.arange(n_cmp, dtype=I32)
    dc = r[:, None] - CMP_BLOCK * (j[None, :] - 4) - (CMP_BLOCK - 1)
    fc = jnp.where((j < 8)[None, :, None], _bias_of_distance(rel_bias, dc), rel_bias[N_BUCKETS - 1][None, None, :])
    return tz, fc.transpose(2, 0, 1)


def _prompt_attention(sb_q, sb_kv, nsa_q, cmp_kv, sel_kv, win_kv, gates, w_cmp_pos, rel_bias, b, t):
    n_cmp = t // CMP_BLOCK
    tz, fc = _bias_tables(rel_bias, n_cmp)
    wt = jnp.tile(w_cmp_pos, (1, n_cmp))
    o_sb = _sb_prompt(sb_q, sb_kv, b, t)
    o_cmp, imp = _cmp_prompt(nsa_q, cmp_kv, wt, fc, b, t)
    o_nsa = _selwin_prompt(nsa_q, imp, gates, o_cmp, sel_kv, win_kv, tz, b, t)
    return o_sb, o_nsa


def _prep_weights(w_in, w_out, w_router, b_router, l):
    pad_in = sum(_SPLITS) - w_in.shape[2]
    pad_r = LANES - N_EXPERTS
    return dict(w_in_b=jnp.pad(w_in[l], ((0, 0), (0, pad_in))).astype(BF16), w_out_b=w_out[l].astype(BF16),
                w_r=jnp.pad(w_router[l], ((0, 0), (0, pad_r))), b_r=jnp.pad(b_router[l], (0, pad_r)))


def _tile(n, want):
    while n % want:
        want //= 2
    return want


def _prompt_layer(x_prompt, mods, w, l):
    b, t, d = x_prompt.shape
    n = b * t
    x = x_prompt.reshape(n, d)
    sh_a, sc_a, gt_a, sh_f, sc_f, gt_f = [m.reshape(b, 1, d) for m in mods]
    sb_q, sb_kv, nsa_q, cmp_kv, sel_kv, win_kv, gates = _proj_in(
        x, w['g_attn_pre'][l], sh_a, sc_a, w['w_in_b'], t, _tile(t, 512))
    o_sb, o_nsa = _prompt_attention(sb_q, sb_kv, nsa_q, cmp_kv, sel_kv, win_kv, gates, w['w_cmp_pos'][l],
                                    w['rel_bias'], b, t)
    xa, hp, logits = _merge(x, o_sb, o_nsa, gt_a, sh_f, sc_f, w['g_sb_out'][l], w['g_nsa_out'][l], w['g_attn_post'][l],
                            w['g_ffn_pre'][l], w['w_out_b'], w['w_r'], w['b_r'], t, _tile(t, 256))
    y = _moe(hp, logits, xa, gt_f, w['g_ffn_post'][l], w['w_gu'][l], w['b_gu'][l], w['w_dn'][l], w['b_dn'][l],
             t, 512, _tile(n, 512), _tile(t, 256))
    win_buf = min(WINDOW, t)
    kv5 = lambda a, heads: a.reshape(b, t, 2, heads, HEAD_DIM)
    return (y.reshape(b, t, d), kv5(sb_kv, H_SB), kv5(cmp_kv, G_NSA), kv5(sel_kv, G_NSA),
            kv5(win_kv, G_NSA)[:, t - win_buf:])


def _sample_bias_tables(rel_bias, past, ts, n_pages, ncp, w):
    t = jnp.repeat(jnp.arange(ts, dtype=I32), H_NSA)
    h = jnp.tile(jnp.arange(H_NSA, dtype=I32), ts)
    own_head = (h[:, None] == jnp.arange(H_NSA, dtype=I32)).astype(F32)[:, None, :]
    look = lambda d: jnp.sum(_bias_of_distance(rel_bias, d) * own_head, axis=-1)
    pos = past + t[:, None]
    bias_c = look(pos - (jnp.arange(ncp, dtype=I32)[None, :] * CMP_BLOCK + CMP_BLOCK - 1))
    bias_s = look(pos - jnp.arange((n_pages + 1) * QB, dtype=I32)[None, :])
    bias_s = bias_s.reshape(ts * H_NSA, n_pages + 1, QB).transpose(1, 0, 2)
    bias_w = look(w + t[:, None] - jnp.arange(w + QB, dtype=I32)[None, :])
    return bias_c, bias_s, bias_w


def _sample_layer(x_sample, mods, caches, state_win, page_table, w, l):
    db, ts, d = x_sample.shape
    ns = db * ts
    cache_sb, cache_cmp, cache_sel = [_keys_minor(c) for c in caches]
    n_pages, page = page_table.shape[1], cache_sb.shape[-1]
    past = n_pages * page
    assert page == QB and ts <= 8
    x = x_sample.reshape(ns, d)
    tm = _tile(ns, 256)
    sh_a, sc_a, gt_a, sh_f, sc_f, gt_f = [jnp.repeat(m, ts, axis=0).reshape(ns // tm, tm, d) for m in mods]
    sb_q, sb_kv, nsa_q, cmp_kv, sel_kv, win_kv, gates = _proj_in(x, w['g_attn_pre'][l], sh_a, sc_a, w['w_in_b'], ts, tm)
    per_seq = lambda a: a.reshape(db, ts, a.shape[1])
    new_rows = lambda a: jnp.pad(per_seq(a), ((0, 0), (0, QB - ts), (0, 0)))
    pp = _tile(n_pages, 8)
    o_sb = _sb_sample(page_table, per_seq(sb_q), new_rows(sb_kv), cache_sb, l)
    total = past + ts + (-(past + ts)) % SEL_BLOCK
    ncp = -(-(total // CMP_BLOCK + 1) // LANES) * LANES
    wbuf = state_win.shape[1]
    bias_c, bias_s, bias_w = _sample_bias_tables(w['rel_bias'], past, ts, n_pages, ncp, wbuf)
    wrow = jnp.tile(w['w_cmp_pos'][l], (1, QB // CMP_BLOCK))
    nq = per_seq(nsa_q)
    o_cmp, sel32 = _cmp_sample(page_table, nq, new_rows(cmp_kv), cache_cmp, l, wrow, bias_c, past, pp)
    o_nsa = _selwin_sample(page_table, nq, new_rows(sel_kv), sel32, bias_s, o_cmp, per_seq(gates),
                           state_win.reshape(db, wbuf, 2 * D_KV), new_rows(win_kv), bias_w, cache_sel, l, pp)
    flat = lambda o: o[:, :ts].reshape(ns, o.shape[2])
    xa, hp, logits = _merge(x, flat(o_sb), flat(o_nsa), gt_a, sh_f, sc_f, w['g_sb_out'][l], w['g_nsa_out'][l],
                            w['g_attn_post'][l], w['g_ffn_pre'][l], w['w_out_b'], w['w_r'], w['b_r'], ts, tm)
    y = _moe(hp, logits, xa, gt_f, w['g_ffn_post'][l], w['w_gu'][l], w['b_gu'][l], w['w_dn'][l], w['b_dn'][l],
             ts, 128, _tile(ns, 512), tm)
    kv5 = lambda a, heads: a.reshape(db, ts, 2, heads, HEAD_DIM)
    win_new = jnp.concatenate([state_win, kv5(win_kv, G_NSA)], axis=1)[:, ts:]
    return y.reshape(db, ts, d), kv5(sb_kv, H_SB), kv5(cmp_kv, G_NSA), kv5(sel_kv, G_NSA), win_new


def kernel(x_prompt, x_sample, cache_sb_kv, cache_cmp_kv, cache_sel_kv, state_win_kv, page_table, c_prompt, c_sample, w_ada, b_ada, g_attn_pre, g_attn_post, w_in, w_cmp_pos, rel_bias, g_sb_out, g_nsa_out, w_out, g_ffn_pre, g_ffn_post, w_router, b_router, w_gu, b_gu, w_dn, b_dn):
    depth = w_in.shape[0]
    assert depth == 1
    b = x_prompt.shape[0]
    xp, xs = x_prompt, x_sample
    outs = [[] for _ in range(8)]
    for l in range(depth):
        w = _prep_weights(w_in, w_out, w_router, b_router, l)
        w.update(g_attn_pre=g_attn_pre, g_attn_post=g_attn_post, w_cmp_pos=w_cmp_pos, rel_bias=rel_bias, g_sb_out=g_sb_out,
                 g_nsa_out=g_nsa_out, g_ffn_pre=g_ffn_pre, g_ffn_post=g_ffn_post, w_gu=w_gu, b_gu=b_gu, w_dn=w_dn, b_dn=b_dn)
        mod = _adaln(jnp.concatenate([c_prompt, c_sample], axis=0), w_ada[l], b_ada[l])
        xp, sb_p, cmp_p, sel_p, win_p = _prompt_layer(xp, jnp.split(mod[:b], 6, axis=-1), w, l)
        xs, sb_s, cmp_s, sel_s, win_s = _sample_layer(
            xs, jnp.split(mod[b:], 6, axis=-1), (cache_sb_kv, cache_cmp_kv, cache_sel_kv), state_win_kv[l],
            page_table, w, l)
        for o, v in zip(outs, (sb_p, sb_s, cmp_p, cmp_s, sel_p, sel_s, win_p, win_s)):
            o.append(v)
    return (xp, xs) + tuple(jnp.stack(o) for o in outs)
```
